```python
import jax, jax.numpy as jnp
from jax import lax
import numpy as np

D_MODEL = 1024
BATCH = 2
SEQ = 8192
DEPTH = 1
DEC_BATCH = 128
DEC_SEQ = 1
PAST_LEN = 2048
PAGE_SIZE = 128

N_HEADS = 16
HEAD_DIM = 64
KV_HEADS = 4
Q_PER_KV = N_HEADS // KV_HEADS
ATTN_SCALE = HEAD_DIM ** -0.5
IDX_HEADS = 8
IDX_DIM = 64
IDX_SCALE = IDX_DIM ** -0.5
IDX_W_SCALE = IDX_HEADS ** -0.5
TOPK_MAX = 256
QBLK = 128
D_SGU = D_MODEL
SGU_GROUPS = 8
SGU_CH = D_SGU // SGU_GROUPS
CHUNK = 128
D_FF = 4 * D_MODEL
PLE_DIM = 256
EPS = 1e-6
IN_SPLITS = (N_HEADS * HEAD_DIM, KV_HEADS * HEAD_DIM, KV_HEADS * HEAD_DIM,
             IDX_HEADS * IDX_DIM, IDX_DIM, IDX_HEADS, D_SGU, D_SGU, D_MODEL, D_MODEL)
IN_WIDTH = sum(IN_SPLITS)

kernel_name = "dsa_gmlp_gated_hybrid_step"


def rmsnorm(x, g):
    xf = x.astype(jnp.float32)
    y = xf * lax.rsqrt(jnp.mean(xf * xf, axis=-1, keepdims=True) + EPS)
    return (y * g.astype(jnp.float32)).astype(x.dtype)


def project_in(x, g_mix, w_in, g_q, g_k):
    B, T, _ = x.shape
    z = rmsnorm(x, g_mix) @ w_in
    pieces = []
    start = 0
    for n in IN_SPLITS:
        pieces.append(z[..., start:start + n])
        start += n
    q, k, v, qi, ki, wi, u, vb, ga, gb = pieces
    q = rmsnorm(q.reshape(B, T, N_HEADS, HEAD_DIM), g_q) * ATTN_SCALE
    k = rmsnorm(k.reshape(B, T, KV_HEADS, HEAD_DIM), g_k)
    v = v.reshape(B, T, KV_HEADS, HEAD_DIM)
    qi = qi.reshape(B, T, IDX_HEADS, IDX_DIM) * IDX_SCALE
    wi = wi * IDX_W_SCALE
    u = jax.nn.gelu(u)
    vb = jax.nn.gelu(vb)
    return q, k, v, qi, ki, wi, u, vb, ga, gb


def dsa_attend(q, qi, wi, qpos, k_all, v_all, ki_all):
    B, Q = q.shape[:2]
    L = k_all.shape[1]
    topk = min(TOPK_MAX, L // 4)
    logits = jnp.einsum("bqhd,bld->bqhl", qi, ki_all)
    score = jnp.einsum("bqhl,bqh->bql", jax.nn.relu(logits), wi).astype(jnp.float32)
    allowed = jnp.arange(L, dtype=jnp.int32)[None, :] <= qpos[:, None]
    score = jnp.where(allowed[None], score, -jnp.inf)
    _, idx = lax.top_k(score, topk)
    valid = idx <= qpos[None, :, None]
    gather = jax.vmap(lambda a, i: a[i])
    kg = gather(k_all, idx)
    vg = gather(v_all, idx)
    qg = q.reshape(B, Q, KV_HEADS, Q_PER_KV, HEAD_DIM)
    s = jnp.einsum("bqcgd,bqkcd->bqcgk", qg, kg).astype(jnp.float32)
    s = jnp.where(valid[:, :, None, None, :], s, -jnp.inf)
    p = jax.nn.softmax(s, axis=-1).astype(vg.dtype)
    o = jnp.einsum("bqcgk,bqkcd->bqcgd", p, vg)
    return o.reshape(B, Q, N_HEADS * HEAD_DIM)


def prompt_attention(q, qi, wi, k, v, ki):
    B, S = q.shape[:2]
    nblk = S // QBLK

    def blocks(a):
        return jnp.moveaxis(a.reshape(B, nblk, QBLK, *a.shape[2:]), 1, 0)

    qpos = jnp.arange(S, dtype=jnp.int32).reshape(nblk, QBLK)
    out = lax.map(lambda args: dsa_attend(args[0], args[1], args[2], args[3], k, v, ki),
                  (blocks(q), blocks(qi), blocks(wi), qpos))
    return jnp.moveaxis(out, 0, 1).reshape(B, S, N_HEADS * HEAD_DIM)


def sgu(u, v, g_v, w_s, b_s):
    B, T, _ = v.shape
    vn = rmsnorm(v, g_v)
    vc = vn.reshape(B, T // CHUNK, CHUNK, SGU_GROUPS, SGU_CH)
    tril = jnp.tril(jnp.ones((CHUNK, CHUNK), dtype=bool))
    ws = jnp.where(tril[None], w_s, jnp.zeros_like(w_s))
    mixed = jnp.einsum("bnsgc,gts->bntgc", vc, ws) + jnp.transpose(b_s)[None, None, :, :, None]
    return vn, u * mixed.reshape(B, T, D_SGU)


def finish(x, o_att, o_sgu, ga, gb, w_o, g_ffn, w_up, w_down, p, g_ple, w_pg, w_p):
    m = jax.nn.sigmoid(ga) * o_att + jax.nn.sigmoid(gb) * o_sgu
    x = x + m @ w_o
    hf = rmsnorm(x, g_ffn)
    x = x + jnp.square(jax.nn.relu(hf @ w_up)) @ w_down
    hp = rmsnorm(x, g_ple)
    x = x + jax.nn.sigmoid(hp @ w_pg) * (p @ w_p)
    return x


def gather_pages(cache, page_table):
    pages = cache[page_table]
    nb, npg, ps = pages.shape[:3]
    return pages.reshape(nb, npg * ps, *cache.shape[2:])


def setup_inputs(seed: int = 0) -> dict:
    key = jax.random.key(seed)
    ks = jax.random.split(key, 24)
    n_pages = PAST_LEN // PAGE_SIZE
    n_phys = (5 * DEC_BATCH * n_pages + 3) // 4
    nrm = jax.random.normal
    f32 = jnp.float32
    page_table = jax.random.permutation(ks[5], n_phys)[:DEC_BATCH * n_pages]
    page_table = page_table.reshape(DEC_BATCH, n_pages).astype(jnp.int32)
    return {
        "x_prompt": nrm(ks[0], (BATCH, SEQ, D_MODEL), f32),
        "x_sample": nrm(ks[1], (DEC_BATCH, DEC_SEQ, D_MODEL), f32),
        "cache_k": nrm(ks[2], (DEPTH, n_phys, PAGE_SIZE, KV_HEADS, HEAD_DIM), f32),
        "cache_v": nrm(ks[3], (DEPTH, n_phys, PAGE_SIZE, KV_HEADS, HEAD_DIM), f32),
        "cache_kidx": nrm(ks[4], (DEPTH, n_phys, PAGE_SIZE, IDX_DIM), f32),
        "page_table": page_table,
        "p_prompt": nrm(ks[6], (DEPTH, BATCH, SEQ, PLE_DIM), f32),
        "p_sample": nrm(ks[7], (DEPTH, DEC_BATCH, DEC_SEQ, PLE_DIM), f32),
        "g_mix": 1.0 + 0.02 * nrm(ks[8], (DEPTH, D_MODEL), f32),
        "w_in": nrm(ks[9], (DEPTH, D_MODEL, IN_WIDTH), f32) * D_MODEL ** -0.5,
        "g_q": 1.0 + 0.02 * nrm(ks[10], (DEPTH, HEAD_DIM), f32),
        "g_k": 1.0 + 0.02 * nrm(ks[11], (DEPTH, HEAD_DIM), f32),
        "g_sgu": 1.0 + 0.02 * nrm(ks[12], (DEPTH, D_SGU), f32),
        "w_s": nrm(ks[13], (DEPTH, SGU_GROUPS, CHUNK, CHUNK), f32) * CHUNK ** -0.5,
        "b_s": 1.0 + 0.01 * nrm(ks[14], (DEPTH, SGU_GROUPS, CHUNK), f32),
        "w_o": nrm(ks[15], (DEPTH, D_MODEL, D_MODEL), f32) * D_MODEL ** -0.5,
        "g_ffn": 1.0 + 0.02 * nrm(ks[16], (DEPTH, D_MODEL), f32),
        "w_up": nrm(ks[17], (DEPTH, D_MODEL, D_FF), f32) * D_MODEL ** -0.5,
        "w_down": nrm(ks[18], (DEPTH, D_FF, D_MODEL), f32) * D_FF ** -0.5,
        "g_ple": 1.0 + 0.02 * nrm(ks[19], (DEPTH, D_MODEL), f32),
        "w_pg": nrm(ks[20], (DEPTH, D_MODEL, D_MODEL), f32) * D_MODEL ** -0.5,
        "w_p": nrm(ks[21], (DEPTH, PLE_DIM, D_MODEL), f32) * PLE_DIM ** -0.5,
    }


def reference(x_prompt, x_sample, cache_k, cache_v, cache_kidx, page_table, p_prompt, p_sample,
              g_mix, w_in, g_q, g_k, g_sgu, w_s, b_s, w_o, g_ffn, w_up, w_down, g_ple, w_pg, w_p):
    xp = x_prompt
    xs = x_sample
    past_len = page_table.shape[1] * cache_k.shape[2]
    dec_seq = x_sample.shape[1]
    pad = (-dec_seq) % CHUNK
    kp_l, vp_l, kip_l, sp_l = [], [], [], []
    ks_l, vs_l, kis_l, ss_l = [], [], [], []
    for i in range(DEPTH):
        q, k, v, qi, ki, wi, u, vb, ga, gb = project_in(xp, g_mix[i], w_in[i], g_q[i], g_k[i])
        o_att = prompt_attention(q, qi, wi, k, v, ki)
        vn, o_sgu = sgu(u, vb, g_sgu[i], w_s[i], b_s[i])
        xp = finish(xp, o_att, o_sgu, ga, gb, w_o[i], g_ffn[i], w_up[i], w_down[i],
                    p_prompt[i], g_ple[i], w_pg[i], w_p[i])
        kp_l.append(k); vp_l.append(v); kip_l.append(ki); sp_l.append(vn)
        q, k, v, qi, ki, wi, u, vb, ga, gb = project_in(xs, g_mix[i], w_in[i], g_q[i], g_k[i])
        k_all = jnp.concatenate([gather_pages(cache_k[i], page_table), k], axis=1)
        v_all = jnp.concatenate([gather_pages(cache_v[i], page_table), v], axis=1)
        ki_all = jnp.concatenate([gather_pages(cache_kidx[i], page_table), ki], axis=1)
        qpos = past_len + jnp.arange(dec_seq, dtype=jnp.int32)
        o_att = dsa_attend(q, qi, wi, qpos, k_all, v_all, ki_all)
        u_p = jnp.pad(u, ((0, 0), (0, pad), (0, 0)))
        vb_p = jnp.pad(vb, ((0, 0), (0, pad), (0, 0)))
        vn, o_sgu = sgu(u_p, vb_p, g_sgu[i], w_s[i], b_s[i])
        vn = vn[:, :dec_seq]
        o_sgu = o_sgu[:, :dec_seq]
        xs = finish(xs, o_att, o_sgu, ga, gb, w_o[i], g_ffn[i], w_up[i], w_down[i],
                    p_sample[i], g_ple[i], w_pg[i], w_p[i])
        ks_l.append(k); vs_l.append(v); kis_l.append(ki); ss_l.append(vn)
    new_k_prompt = jnp.stack(kp_l)
    new_v_prompt = jnp.stack(vp_l)
    new_kidx_prompt = jnp.stack(kip_l)
    new_sgu_v_prompt = jnp.stack(sp_l)
    new_k_sample = jnp.stack(ks_l)
    new_v_sample = jnp.stack(vs_l)
    new_kidx_sample = jnp.stack(kis_l)
    new_sgu_v_sample = jnp.stack(ss_l)
    return (xp, xs, new_k_prompt, new_v_prompt, new_kidx_prompt, new_sgu_v_prompt,
            new_k_sample, new_v_sample, new_kidx_sample, new_sgu_v_sample)
```

```python
import functools

import jax
import jax.numpy as jnp
from jax import lax
from jax.experimental import pallas as pl
from jax.experimental.pallas import tpu as pltpu

F32 = jnp.float32
BF16 = jnp.bfloat16
I32 = jnp.int32

N_HEADS = 16
HEAD_DIM = 64
HEAD_SHIFT = 6
KV_HEADS = 4
Q_PER_KV = N_HEADS // KV_HEADS
IDX_HEADS = 8
IDX_DIM = 64
TOPK_MAX = 256
QBLK = 128
CHUNK = 128
SGU_GROUPS = 8
EPS = 1e-6
ATTN_SCALE = HEAD_DIM ** -0.5
IDX_SCALE = IDX_DIM ** -0.5
IDX_W_SCALE = IDX_HEADS ** -0.5

LANES = 128
INT_MIN = -(2 ** 31)
INT_MAX = 2 ** 31 - 1
KEY_ABOVE_INF = 0x7F800001
MASK_BIAS = -1e30
M_INIT = -1e29
VMEM_LIMIT = 56 * 1024 * 1024


def _dot(a, b):
    return jnp.dot(a, b, preferred_element_type=F32)


def _dot_t(a, b):
    return lax.dot_general(a, b, (((1,), (1,)), ((), ())), preferred_element_type=F32)


def _hi_lo(x):
    hi = x.astype(BF16)
    lo = (x - hi.astype(F32)).astype(BF16)
    return hi, lo


def _sigmoid(x):
    return 1.0 / (1.0 + jnp.exp(-x))


def _rms_rows(x, g):
    return x * lax.rsqrt(jnp.mean(x * x, axis=-1, keepdims=True) + EPS) * g


def _head_rms_scale(z):
    w = z.shape[1]
    seg = lax.broadcasted_iota(I32, (w, LANES), 0) >> HEAD_SHIFT
    col = lax.broadcasted_iota(I32, (w, LANES), 1)
    g = (seg == col).astype(BF16)
    row_t = lax.broadcasted_iota(I32, (LANES, w), 0)
    seg_t = lax.broadcasted_iota(I32, (LANES, w), 1) >> HEAD_SHIFT
    g_t = (row_t == seg_t).astype(BF16)
    hi, lo = _hi_lo(z * z)
    ssq = _dot(hi, g) + _dot(lo, g)
    r = lax.rsqrt(ssq * (1.0 / HEAD_DIM) + EPS)
    rh, rl = _hi_lo(r)
    return _dot(rh, g_t) + _dot(rl, g_t)


def _to_key(x):
    b = lax.bitcast_convert_type(x, I32)
    return b ^ ((b >> 31) & INT_MAX)


_C_Q, _C_K, _C_V, _C_QI, _C_KI, _C_WI, _C_U, _C_VB, _C_GA, _C_GB, _C_END = (
    0, 1024, 1280, 1536, 2048, 2176, 2304, 3328, 4352, 5376, 6400)


def _proj_kernel(x_ref, w_ref, gmix_ref, gq_ref, gk_ref, gsgu_ref, ws_ref, bs_ref,
                 q_ref, kf_ref, kb_ref, vf_ref, vb_ref, qi3_ref, kif_ref, ki3_ref, wi_ref,
                 vn_ref, sga_ref, sgu_ref, *, sample):
    tm = x_ref.shape[0]
    xn = _rms_rows(x_ref[...], gmix_ref[...]).astype(BF16)

    def sec(lo, hi):
        return _dot(xn, w_ref[:, lo:hi])

    zq = sec(_C_Q, _C_K)
    q_ref[...] = (zq * _head_rms_scale(zq) * (gq_ref[...] * ATTN_SCALE)).astype(BF16)
    zk = sec(_C_K, _C_V)
    kn = zk * _head_rms_scale(zk) * gk_ref[...]
    kf_ref[...] = kn
    kb_ref[...] = kn.astype(BF16)
    zv = sec(_C_V, _C_QI)
    vf_ref[...] = zv
    vb_ref[...] = zv.astype(BF16)

    first = lax.broadcasted_iota(I32, (tm, LANES), 1) < IDX_DIM
    zero = jnp.zeros((tm, LANES), BF16)
    zqi = sec(_C_QI, _C_KI) * IDX_SCALE
    for pair in range(IDX_HEADS // 2):
        v = zqi[:, pair * LANES:(pair + 1) * LANES]
        r = pltpu.roll(v, IDX_DIM, 1)
        hv, lv = _hi_lo(v)
        hr, lr = _hi_lo(r)
        qi3_ref[2 * pair, :, 0:LANES] = jnp.where(first, hv, lr)
        qi3_ref[2 * pair, :, LANES:2 * LANES] = jnp.where(first, hv, zero)
        qi3_ref[2 * pair + 1, :, 0:LANES] = jnp.where(first, hr, lv)
        qi3_ref[2 * pair + 1, :, LANES:2 * LANES] = jnp.where(first, hr, zero)
    zki = sec(_C_KI, _C_WI)
    kif_ref[...] = zki[:, 0:IDX_DIM]
    hk, lk = _hi_lo(zki)
    ki3_ref[:, 0:LANES] = hk
    ki3_ref[:, LANES:2 * LANES] = jnp.where(first, lk, zero)
    wi_ref[...] = sec(_C_WI, _C_U) * IDX_W_SCALE

    u = jax.nn.gelu(sec(_C_U, _C_VB))
    vn = _rms_rows(jax.nn.gelu(sec(_C_VB, _C_GA)), gsgu_ref[...])
    vn_ref[...] = vn
    sga_ref[...] = _sigmoid(sec(_C_GA, _C_GB))
    sgb = _sigmoid(sec(_C_GB, _C_END))
    if sample:
        sgu_ref[...] = sgb * (u * (vn * ws_ref[...] + bs_ref[...]))
    else:
        tril = (lax.broadcasted_iota(I32, (CHUNK, CHUNK), 0)
                >= lax.broadcasted_iota(I32, (CHUNK, CHUNK), 1))
        for g in range(SGU_GROUPS):
            wg = jnp.where(tril, ws_ref[g], 0.0).astype(BF16)
            bg = bs_ref[:, g:g + 1]
            for n in range(tm // CHUNK):
                rows = slice(n * CHUNK, (n + 1) * CHUNK)
                cols = slice(g * CHUNK, (g + 1) * CHUNK)
                mixed = _dot(wg, vn[rows, cols].astype(BF16)) + bg
                sgu_ref[rows, cols] = sgb[rows, cols] * (u[rows, cols] * mixed)


def _proj(x, w_all, gmix, gq, gk, gsgu, ws, bs, *, sample, tm):
    t = x.shape[0]
    d = x.shape[1]
    grid = (t // tm,)
    row = lambda i: (i, 0)
    const2 = lambda i: (0, 0)
    if sample:
        ws_spec = pl.BlockSpec(ws.shape, const2)
        bs_spec = pl.BlockSpec(bs.shape, const2)
    else:
        ws_spec = pl.BlockSpec(ws.shape, lambda i: (0, 0, 0))
        bs_spec = pl.BlockSpec(bs.shape, const2)
    out_shapes = (
        jax.ShapeDtypeStruct((t, 1024), BF16),
        jax.ShapeDtypeStruct((t, 256), F32), jax.ShapeDtypeStruct((t, 256), BF16),
        jax.ShapeDtypeStruct((t, 256), F32), jax.ShapeDtypeStruct((t, 256), BF16),
        jax.ShapeDtypeStruct((IDX_HEADS, t, 256), BF16),
        jax.ShapeDtypeStruct((t, IDX_DIM), F32),
        jax.ShapeDtypeStruct((t, 256), BF16),
        jax.ShapeDtypeStruct((t, LANES), F32),
        jax.ShapeDtypeStruct((t, 1024), F32),
        jax.ShapeDtypeStruct((t, 1024), F32),
        jax.ShapeDtypeStruct((t, 1024), F32),
    )
    out_specs = (
        pl.BlockSpec((tm, 1024), row),
        pl.BlockSpec((tm, 256), row), pl.BlockSpec((tm, 256), row),
        pl.BlockSpec((tm, 256), row), pl.BlockSpec((tm, 256), row),
        pl.BlockSpec((IDX_HEADS, tm, 256), lambda i: (0, i, 0)),
        pl.BlockSpec((tm, IDX_DIM), row),
        pl.BlockSpec((tm, 256), row),
        pl.BlockSpec((tm, LANES), row),
        pl.BlockSpec((tm, 1024), row),
        pl.BlockSpec((tm, 1024), row),
        pl.BlockSpec((tm, 1024), row),
    )
    return pl.pallas_call(
        functools.partial(_proj_kernel, sample=sample),
        grid=grid,
        in_specs=[
            pl.BlockSpec((tm, d), row),
            pl.BlockSpec(w_all.shape, const2, pipeline_mode=pl.Buffered(1)),
            pl.BlockSpec((1, d), const2), pl.BlockSpec((1, 1024), const2),
            pl.BlockSpec((1, 256), const2), pl.BlockSpec((1, 1024), const2),
            ws_spec, bs_spec,
        ],
        out_specs=out_specs,
        out_shape=out_shapes,
        compiler_params=pltpu.CompilerParams(
            dimension_semantics=("arbitrary",), vmem_limit_bytes=VMEM_LIMIT),
        name="proj_sample" if sample else "proj_prompt",
    )(x, w_all, gmix, gq, gk, gsgu, ws, bs)


def _select(key_ref, n_tiles, topk):
    rows = key_ref.shape[0]
    topk_f = float(topk)

    def count(pred):
        def body(t, acc):
            start = pl.multiple_of(t * LANES, LANES)
            tile = key_ref[:, pl.ds(start, LANES)]
            return acc + jnp.where(pred(tile, t), 1.0, 0.0)
        acc = lax.fori_loop(0, n_tiles, body, jnp.zeros((rows, LANES), F32))
        return jnp.sum(acc, axis=1, keepdims=True)

    def count_ge(thr):
        thr_b = jnp.broadcast_to(thr, (rows, LANES))
        return count(lambda tile, t: tile >= thr_b)

    def cond(st):
        return jnp.logical_and(st[4] > 0.5, st[5] < 40)

    def body(st):
        lo, hi, cnt_lo, cnt_hi, _, it = st
        mid = (lo >> 1) + (hi >> 1) + (lo & hi & 1)
        c = count_ge(mid)
        ge = c >= topk_f
        exact = c == topk_f
        lo2 = jnp.where(ge, mid, lo)
        cnt_lo2 = jnp.where(ge, c, cnt_lo)
        hi2 = jnp.where(exact, mid + 1, jnp.where(ge, hi, mid))
        cnt_hi2 = jnp.where(ge, cnt_hi, c)
        active = jnp.where(hi2 != lo2 + 1, 1.0, 0.0)
        return lo2, hi2, cnt_lo2, cnt_hi2, jnp.max(active), it + 1

    init = (jnp.full((rows, 1), INT_MIN, I32), jnp.full((rows, 1), KEY_ABOVE_INF, I32),
            jnp.full((rows, 1), 2.0 * topk_f, F32), jnp.zeros((rows, 1), F32),
            jnp.float32(1.0), jnp.int32(0))
    lo, _, cnt_lo, cnt_hi, _, _ = lax.while_loop(cond, body, init)

    tie = jnp.logical_and(cnt_lo > topk_f, lo > INT_MIN)
    need = topk_f - cnt_hi
    any_tie = jnp.max(jnp.where(tie, 1.0, 0.0))

    def tie_search():
        lo_b = jnp.broadcast_to(lo, (rows, LANES))
        lane = lax.broadcasted_iota(I32, (rows, LANES), 1)

        def step(_, st):
            jlo, jhi = st
            mid = (jlo + jhi) >> 1
            mid_b = jnp.broadcast_to(mid, (rows, LANES))
            f = count(lambda tile, t: jnp.logical_and(tile == lo_b, lane + t * LANES <= mid_b))
            ok = f >= need
            return jnp.where(ok, jlo, mid), jnp.where(ok, mid, jhi)

        n_steps = max(1, (key_ref.shape[1]).bit_length())
        _, jhi = lax.fori_loop(
            0, n_steps, step,
            (jnp.full((rows, 1), -1, I32), jnp.broadcast_to(n_tiles * LANES - 1, (rows, 1)).astype(I32)))
        return jnp.where(tie, jhi, INT_MAX)

    jcut = lax.cond(any_tie > 0.5, tie_search, lambda: jnp.full((rows, 1), INT_MAX, I32))
    tau = jnp.maximum(lo, INT_MIN + 1)
    return tau, jcut


def _selected(keys, kpos, tau, jcut):
    return jnp.logical_or(keys > tau, jnp.logical_and(keys == tau, kpos <= jcut))


def _attn_kernel(q_ref, qi3_ref, wi_ref, k_ref, v_ref, ki3_ref, o_ref,
                 key_s, wexp_s, lhs_s, m_s, l_s, acc_s, *, topk, ck):
    i = pl.program_id(1)
    n_ck = (i * QBLK + QBLK - 1) // ck + 1
    tiles_per_ck = ck // LANES
    lane = lax.broadcasted_iota(I32, (QBLK, LANES), 1)
    qpos = lax.broadcasted_iota(I32, (QBLK, LANES), 0) + i * QBLK

    qi3 = qi3_ref[...].reshape(IDX_HEADS * QBLK, 2 * LANES)
    for h in range(IDX_HEADS):
        wexp_s[h] = jnp.broadcast_to(wi_ref[:, h:h + 1], (QBLK, LANES))
    lane256 = lax.broadcasted_iota(I32, (QBLK, 2 * LANES), 1) >> HEAD_SHIFT
    for g in range(Q_PER_KV):
        qg = q_ref[:, g * 256:(g + 1) * 256]
        for c in range(KV_HEADS):
            r0 = (g * KV_HEADS + c) * QBLK
            lhs_s[r0:r0 + QBLK, :] = jnp.where(lane256 == c, qg, jnp.zeros_like(qg))
    m_s[...] = jnp.full(m_s.shape, M_INIT, F32)
    l_s[...] = jnp.zeros(l_s.shape, F32)
    acc_s[...] = jnp.zeros(acc_s.shape, F32)

    def score_chunk(j, carry):
        base = pl.multiple_of(j * ck, ck)
        lg = _dot_t(qi3, ki3_ref[pl.ds(base, ck), :])
        for lt in range(tiles_per_ck):
            acc = jnp.zeros((QBLK, LANES), F32)
            for h in range(IDX_HEADS):
                acc = acc + jnp.maximum(lg[h * QBLK:(h + 1) * QBLK, lt * LANES:(lt + 1) * LANES], 0.0) * wexp_s[h]
            kpos = lane + (base + lt * LANES)
            key_s[:, pl.ds(pl.multiple_of(base + lt * LANES, LANES), LANES)] = jnp.where(
                kpos <= qpos, _to_key(acc), INT_MIN)
        return carry

    lax.fori_loop(0, n_ck, score_chunk, 0)

    tau, jcut = _select(key_s, n_ck * tiles_per_ck, topk)

    kpos0 = lax.broadcasted_iota(I32, (QBLK, ck), 1)
    rows_g = KV_HEADS * QBLK

    def attn_chunk(j, carry):
        base = pl.multiple_of(j * ck, ck)
        kc = k_ref[pl.ds(base, ck), :]
        vc = v_ref[pl.ds(base, ck), :]
        sel = _selected(key_s[:, pl.ds(base, ck)], kpos0 + base, tau, jcut)
        bias = jnp.where(sel, 0.0, MASK_BIAS)
        bias = jnp.concatenate([bias] * KV_HEADS, axis=0)
        for g in range(Q_PER_KV):
            rg = slice(g * rows_g, (g + 1) * rows_g)
            s = _dot_t(lhs_s[rg, :], kc) + bias
            m_old = m_s[rg, :]
            m_new = jnp.maximum(m_old, jnp.max(s, axis=1, keepdims=True))
            p = jnp.exp(s - m_new)
            alpha = jnp.exp(m_old - m_new)
            l_s[rg, :] = alpha * l_s[rg, :] + jnp.sum(p, axis=1, keepdims=True)
            m_s[rg, :] = m_new
            pv = _dot(p.astype(BF16), vc)
            for c in range(KV_HEADS):
                r0 = g * rows_g + c * QBLK
                half = (c // 2) * LANES
                acc_s[r0:r0 + QBLK, :] = (alpha[c * QBLK:(c + 1) * QBLK] * acc_s[r0:r0 + QBLK, :]
                                          + pv[c * QBLK:(c + 1) * QBLK, half:half + LANES])
        return carry

    lax.fori_loop(0, n_ck, attn_chunk, 0)

    odd = lane >= HEAD_DIM
    for g in range(Q_PER_KV):
        for hf in range(2):
            ra = (g * KV_HEADS + 2 * hf) * QBLK
            rb = ra + QBLK
            oa = acc_s[ra:ra + QBLK, :] / l_s[ra:ra + QBLK, :]
            ob = acc_s[rb:rb + QBLK, :] / l_s[rb:rb + QBLK, :]
            o_ref[:, g * 256 + hf * LANES:g * 256 + (hf + 1) * LANES] = jnp.where(odd, ob, oa)


def _attn_prompt(q, qi3, wi, kb, vb, ki3, *, topk, ck=512):
    b, s, _ = q.shape
    grid = (b, s // QBLK)
    blk = lambda bi, i: (bi, i, 0)
    seq = lambda bi, i: (bi, 0, 0)
    rows = N_HEADS * QBLK
    return pl.pallas_call(
        functools.partial(_attn_kernel, topk=topk, ck=ck),
        grid=grid,
        in_specs=[
            pl.BlockSpec((None, QBLK, 1024), blk),
            pl.BlockSpec((IDX_HEADS, None, QBLK, 256), lambda bi, i: (0, bi, i, 0)),
            pl.BlockSpec((None, QBLK, LANES), blk),
            pl.BlockSpec((None, s, 256), seq),
            pl.BlockSpec((None, s, 256), seq),
            pl.BlockSpec((None, s, 256), seq),
        ],
        out_specs=pl.BlockSpec((None, QBLK, 1024), blk),
        out_shape=jax.ShapeDtypeStruct((b, s, 1024), F32),
        scratch_shapes=[
            pltpu.VMEM((QBLK, s), I32),
            pltpu.VMEM((IDX_HEADS, QBLK, LANES), F32),
            pltpu.VMEM((rows, 256), BF16),
            pltpu.VMEM((rows, 1), F32), pltpu.VMEM((rows, 1), F32),
            pltpu.VMEM((rows, LANES), F32),
        ],
        compiler_params=pltpu.CompilerParams(
            dimension_semantics=("arbitrary", "arbitrary"), vmem_limit_bytes=VMEM_LIMIT),
        name="attn_prompt",
    )(q, qi3, wi, kb, vb, ki3)


def _dec_score_kernel(pt_ref, qi3_ref, wexp_ref, kis_ref, *rest, n_pages, page):
    page_refs = rest[:n_pages]
    key_ref = rest[n_pages]
    del pt_ref
    n_keys = n_pages * page + LANES
    ki_all = jnp.concatenate([r[...] for r in page_refs]
                             + [jnp.broadcast_to(kis_ref[...], (LANES, IDX_DIM))], axis=0)
    hk, lk = _hi_lo(ki_all)
    q3 = qi3_ref[...]
    qh = q3[:, 0:IDX_DIM]
    ql = q3[:, IDX_DIM:2 * IDX_DIM]
    lg = _dot_t(qh, hk) + _dot_t(ql, hk) + _dot_t(qh, lk)
    score = jnp.zeros((1, n_keys), F32)
    for h in range(IDX_HEADS):
        score = score + jnp.maximum(lg[h:h + 1, :], 0.0) * wexp_ref[h:h + 1, :]
    kpos = lax.broadcasted_iota(I32, (1, n_keys), 1)
    key_ref[...] = jnp.where(kpos <= n_pages * page, _to_key(score), INT_MIN)


def _dec_select_kernel(key_ref, tau_ref, jcut_ref, *, topk):
    tau, jcut = _select(key_ref, key_ref.shape[1] // LANES, topk)
    tau_ref[...] = jnp.broadcast_to(tau, tau_ref.shape)
    jcut_ref[...] = jnp.broadcast_to(jcut, jcut_ref.shape)


def _dec_attn_kernel(pt_ref, tau_ref, jcut_ref, lhs_ref, key_ref, ks_ref, vs_ref, *rest, n_pages, page):
    k_refs = rest[:n_pages]
    v_refs = rest[n_pages:2 * n_pages]
    o_ref = rest[2 * n_pages]
    del pt_ref
    b = pl.program_id(0)
    n_past = n_pages * page
    tau = tau_ref[b]
    jcut = jcut_ref[b]
    lhs = lhs_ref[...]
    keys = key_ref[...]
    kpos = lax.broadcasted_iota(I32, keys.shape, 1)
    bias = jnp.where(_selected(keys, kpos, tau, jcut), 0.0, MASK_BIAS)
    s = jnp.concatenate([_dot_t(lhs, r[...].astype(BF16)) for r in k_refs], axis=1)
    s = s + bias[:, 0:n_past]
    ks = ks_ref[...].astype(BF16).astype(F32)
    s_self = jnp.sum(lhs.astype(F32) * ks, axis=1, keepdims=True) + bias[:, n_past:n_past + 1]
    m = jnp.maximum(jnp.max(s, axis=1, keepdims=True), s_self)
    p = jnp.exp(s - m)
    p_self = jnp.exp(s_self - m)
    l = jnp.sum(p, axis=1, keepdims=True) + p_self
    v_all = jnp.concatenate([r[...].astype(BF16) for r in v_refs], axis=0)
    vs = vs_ref[...].astype(BF16).astype(F32)
    o = (_dot(p.astype(BF16), v_all) + p_self.astype(BF16).astype(F32) * vs) / l
    r_i = lax.broadcasted_iota(I32, o.shape, 0)
    l_i = lax.broadcasted_iota(I32, o.shape, 1)
    o = jnp.where((l_i >> HEAD_SHIFT) == (r_i & (KV_HEADS - 1)), o, 0.0)
    for g in range(Q_PER_KV):
        r0 = g * KV_HEADS
        o_ref[g:g + 1, :] = o[r0:r0 + 1] + o[r0 + 1:r0 + 2] + o[r0 + 2:r0 + 3] + o[r0 + 3:r0 + 4]


def _attn_decode(page_table, q_s, qi3_s, wi_s, kf_s, vf_s, kif_s, cache_k, cache_v, cache_kidx, *, topk):
    n, n_pages = page_table.shape
    n_phys, page = cache_k.shape[0], cache_k.shape[1]
    n_keys = n_pages * page + LANES
    ck2 = cache_k.reshape(n_phys, page, KV_HEADS * HEAD_DIM)
    cv2 = cache_v.reshape(n_phys, page, KV_HEADS * HEAD_DIM)

    def page_spec(width, p):
        return pl.BlockSpec((None, page, width), lambda bi, pt, *_: (pt[bi, p], 0, 0))

    per_tok = lambda bi, *_: (bi, 0, 0)
    qi3_t = jnp.transpose(qi3_s, (1, 0, 2))
    wexp = jnp.broadcast_to(wi_s[:, :IDX_HEADS, None], (n, IDX_HEADS, n_keys))
    c_of_lane = jnp.arange(256) // HEAD_DIM
    qg = q_s.reshape(n, Q_PER_KV, 1, 256)
    lhs = jnp.where(c_of_lane[None, None, None, :] == jnp.arange(KV_HEADS)[None, None, :, None],
                    qg, jnp.zeros_like(qg)).reshape(n, N_HEADS, 256)

    keys = pl.pallas_call(
        functools.partial(_dec_score_kernel, n_pages=n_pages, page=page),
        grid_spec=pltpu.PrefetchScalarGridSpec(
            num_scalar_prefetch=1, grid=(n,),
            in_specs=[pl.BlockSpec((None, IDX_HEADS, 256), per_tok),
                      pl.BlockSpec((None, IDX_HEADS, n_keys), per_tok),
                      pl.BlockSpec((None, 1, IDX_DIM), per_tok)]
            + [page_spec(IDX_DIM, p) for p in range(n_pages)],
            out_specs=pl.BlockSpec((None, 1, n_keys), per_tok)),
        out_shape=jax.ShapeDtypeStruct((n, 1, n_keys), I32),
        compiler_params=pltpu.CompilerParams(dimension_semantics=("arbitrary",), vmem_limit_bytes=VMEM_LIMIT),
        name="dec_score",
    )(page_table, qi3_t, wexp, kif_s.reshape(n, 1, IDX_DIM), *([cache_kidx] * n_pages))

    tau, jcut = pl.pallas_call(
        functools.partial(_dec_select_kernel, topk=topk),
        out_shape=(jax.ShapeDtypeStruct((n, LANES), I32), jax.ShapeDtypeStruct((n, LANES), I32)),
        name="dec_select",
    )(keys.reshape(n, n_keys))

    o = pl.pallas_call(
        functools.partial(_dec_attn_kernel, n_pages=n_pages, page=page),
        grid_spec=pltpu.PrefetchScalarGridSpec(
            num_scalar_prefetch=3, grid=(n,),
            in_specs=[pl.BlockSpec((None, N_HEADS, 256), per_tok),
                      pl.BlockSpec((None, 1, n_keys), per_tok),
                      pl.BlockSpec((None, 1, 256), per_tok),
                      pl.BlockSpec((None, 1, 256), per_tok)]
            + [page_spec(256, p) for p in range(n_pages)]
            + [page_spec(256, p) for p in range(n_pages)],
            out_specs=pl.BlockSpec((None, Q_PER_KV, 256), per_tok)),
        out_shape=jax.ShapeDtypeStruct((n, Q_PER_KV, 256), F32),
        compiler_params=pltpu.CompilerParams(dimension_semantics=("arbitrary",), vmem_limit_bytes=VMEM_LIMIT),
        name="dec_attn",
    )(page_table, tau[:, 0], jcut[:, 0], lhs, keys, kf_s.reshape(n, 1, 256), vf_s.reshape(n, 1, 256),
      *([ck2] * n_pages), *([cv2] * n_pages))
    return o.reshape(n, 1024)


def _finish_kernel(x_ref, oatt_ref, sga_ref, sgu_ref, p_ref, wo_ref, wop_ref, gffn_ref, wup_ref, wdn_ref,
                   gple_ref, wpg_ref, wp_ref, y_ref):
    x = x_ref[...]
    att = (sga_ref[...] * oatt_ref[...]).astype(BF16)
    x = x + _dot(att, wop_ref[...]) + _dot(sgu_ref[...].astype(BF16), wo_ref[...])
    hf = _rms_rows(x, gffn_ref[...]).astype(BF16)
    up = jnp.maximum(_dot(hf, wup_ref[...]), 0.0)
    x = x + _dot((up * up).astype(BF16), wdn_ref[...])
    hp = _rms_rows(x, gple_ref[...]).astype(BF16)
    gate = _sigmoid(_dot(hp, wpg_ref[...]))
    y_ref[...] = x + gate * _dot(p_ref[...].astype(BF16), wp_ref[...])


def _finish(x, oatt, sga, sgu, p, wo, wop, gffn, wup, wdn, gple, wpg, wp, *, tm, name):
    t, d = x.shape
    row = lambda i: (i, 0)
    const = lambda i: (0, 0)

    def wspec(w):
        return pl.BlockSpec(w.shape, const, pipeline_mode=pl.Buffered(1))

    return pl.pallas_call(
        _finish_kernel,
        grid=(t // tm,),
        in_specs=[pl.BlockSpec((tm, d), row), pl.BlockSpec((tm, d), row), pl.BlockSpec((tm, d), row),
                  pl.BlockSpec((tm, d), row), pl.BlockSpec((tm, p.shape[1]), row),
                  wspec(wo), wspec(wop), pl.BlockSpec((1, d), const), wspec(wup), wspec(wdn),
                  pl.BlockSpec((1, d), const), wspec(wpg), wspec(wp)],
        out_specs=pl.BlockSpec((tm, d), row),
        out_shape=jax.ShapeDtypeStruct((t, d), F32),
        compiler_params=pltpu.CompilerParams(
            dimension_semantics=("arbitrary",), vmem_limit_bytes=VMEM_LIMIT),
        name=name,
    )(x, oatt, sga, sgu, p, wo, wop, gffn, wup, wdn, gple, wpg, wp)


def _swap_head_axes_cols(w):
    k = w.shape[0]
    return w.reshape(k, KV_HEADS, Q_PER_KV, HEAD_DIM).transpose(0, 2, 1, 3).reshape(k, N_HEADS * HEAD_DIM)


def _pack_w_in(w_in):
    d = w_in.shape[0]
    splits = (1024, 256, 256, 512, 64, 8, 1024, 1024, 1024, 1024)
    offs = [0]
    for n in splits:
        offs.append(offs[-1] + n)
    wq, wk, wv, wqi, wki, wwi, wu, wvb, wga, wgb = [w_in[:, offs[i]:offs[i + 1]] for i in range(10)]
    packed = jnp.concatenate([
        _swap_head_axes_cols(wq), wk, wv, wqi,
        wki, wki,
        wwi, jnp.zeros((d, LANES - IDX_HEADS), w_in.dtype),
        wu, wvb, _swap_head_axes_cols(wga), wgb], axis=1)
    return packed.astype(BF16)


def kernel(x_prompt, x_sample, cache_k, cache_v, cache_kidx, page_table, p_prompt, p_sample, g_mix, w_in, g_q, g_k, g_sgu, w_s, b_s, w_o, g_ffn, w_up, w_down, g_ple, w_pg, w_p):
    depth = w_in.shape[0]
    bsz, seq, d = x_prompt.shape
    n_dec, dec_seq, _ = x_sample.shape
    assert dec_seq == 1 and seq % QBLK == 0
    n_pages, page = page_table.shape[1], cache_k.shape[2]
    topk_p = min(TOPK_MAX, seq // 4)
    topk_s = min(TOPK_MAX, (n_pages * page + dec_seq) // 4)
    t_p = bsz * seq

    xp = x_prompt.reshape(t_p, d)
    xs = x_sample.reshape(n_dec, d)
    outs = [[] for _ in range(8)]
    for i in range(depth):
        w_all = _pack_w_in(w_in[i])
        gq = jnp.tile(g_q[i], N_HEADS)[None]
        gk = jnp.tile(g_k[i], KV_HEADS)[None]
        gmix = g_mix[i][None]
        gsgu = g_sgu[i][None]
        wo = w_o[i].astype(BF16)
        wop = (w_o[i].reshape(KV_HEADS, Q_PER_KV, HEAD_DIM, d).transpose(1, 0, 2, 3)
               .reshape(N_HEADS * HEAD_DIM, d).astype(BF16))
        fin_w = (wo, wop, g_ffn[i][None], w_up[i].astype(BF16), w_down[i].astype(BF16), g_ple[i][None],
                 w_pg[i].astype(BF16), w_p[i].astype(BF16))

        (q, kf, kb, vf, vb, qi3, kif, ki3, wi, vn, sga, sgu) = _proj(
            xp, w_all, gmix, gq, gk, gsgu, w_s[i], jnp.transpose(b_s[i]), sample=False, tm=256)
        oatt = _attn_prompt(
            q.reshape(bsz, seq, 1024), qi3.reshape(IDX_HEADS, bsz, seq, 256), wi.reshape(bsz, seq, LANES),
            kb.reshape(bsz, seq, 256), vb.reshape(bsz, seq, 256), ki3.reshape(bsz, seq, 256), topk=topk_p)
        xp = _finish(xp, oatt.reshape(t_p, 1024), sga, sgu, p_prompt[i].reshape(t_p, -1), *fin_w,
                     tm=256, name="finish_prompt")
        outs[0].append(kf.reshape(bsz, seq, KV_HEADS, HEAD_DIM))
        outs[1].append(vf.reshape(bsz, seq, KV_HEADS, HEAD_DIM))
        outs[2].append(kif.reshape(bsz, seq, IDX_DIM))
        outs[3].append(vn.reshape(bsz, seq, d))

        w00 = jnp.repeat(w_s[i][:, 0, 0], CHUNK)[None]
        b0 = jnp.repeat(b_s[i][:, 0], CHUNK)[None]
        (q, kf, kb, vf, vb, qi3, kif, ki3, wi, vn, sga, sgu) = _proj(
            xs, w_all, gmix, gq, gk, gsgu, w00, b0, sample=True, tm=n_dec)
        oatt = _attn_decode(page_table, q, qi3, wi, kf, vf, kif, cache_k[i], cache_v[i], cache_kidx[i],
                            topk=topk_s)
        xs = _finish(xs, oatt, sga, sgu, p_sample[i].reshape(n_dec, -1), *fin_w, tm=n_dec,
                     name="finish_sample")
        outs[4].append(kf.reshape(n_dec, 1, KV_HEADS, HEAD_DIM))
        outs[5].append(vf.reshape(n_dec, 1, KV_HEADS, HEAD_DIM))
        outs[6].append(kif.reshape(n_dec, 1, IDX_DIM))
        outs[7].append(vn.reshape(n_dec, 1, d))

    return (xp.reshape(bsz, seq, d), xs.reshape(n_dec, 1, d), *[jnp.stack(o) for o in outs])
```

```python
import functools
import math

import jax
import jax.numpy as jnp
from jax import lax
from jax.experimental import pallas as pl
from jax.experimental.pallas import tpu as pltpu

F32 = jnp.float32
BF16 = jnp.bfloat16
I32 = jnp.int32

N_HEADS = 16
HEAD_DIM = 64
HEAD_SHIFT = 6
KV_HEADS = 4
Q_PER_KV = N_HEADS // KV_HEADS
IDX_HEADS = 8
IDX_DIM = 64
TOPK_MAX = 256
QBLK = 128
CHUNK = 128
SGU_GROUPS = 8
EPS = 1e-6
LOG2E = math.log2(math.e)
ATTN_SCALE = HEAD_DIM ** -0.5
IDX_SCALE = IDX_DIM ** -0.5
IDX_W_SCALE = IDX_HEADS ** -0.5

LANES = 128
SUBLANES = 8
INT_MIN = -(2 ** 31)
INT_MAX = 2 ** 31 - 1
MASK_BIAS = -1e30
M_INIT = -1e29
VMEM_LIMIT = 56 * 1024 * 1024
SEL_ROWS = 256


def _dot(a, b):
    return jnp.dot(a, b, preferred_element_type=F32)


def _dot_t(a, b):
    return lax.dot_general(a, b, (((1,), (1,)), ((), ())), preferred_element_type=F32)


def _hi_lo(x):
    hi = x.astype(BF16)
    lo = (x - hi.astype(F32)).astype(BF16)
    return hi, lo


def _sigmoid(x):
    return 1.0 / (1.0 + jnp.exp(-x))


def _rms_rows(x, g):
    return x * lax.rsqrt(jnp.mean(x * x, axis=-1, keepdims=True) + EPS) * g


def _head_rms_scale(z):
    w = z.shape[1]
    seg = lax.broadcasted_iota(I32, (w, LANES), 0) >> HEAD_SHIFT
    col = lax.broadcasted_iota(I32, (w, LANES), 1)
    g = (seg == col).astype(BF16)
    row_t = lax.broadcasted_iota(I32, (LANES, w), 0)
    seg_t = lax.broadcasted_iota(I32, (LANES, w), 1) >> HEAD_SHIFT
    g_t = (row_t == seg_t).astype(BF16)
    hi, lo = _hi_lo(z * z)
    ssq = _dot(hi, g) + _dot(lo, g)
    r = lax.rsqrt(ssq * (1.0 / HEAD_DIM) + EPS)
    rh, rl = _hi_lo(r)
    return _dot(rh, g_t) + _dot(rl, g_t)


def _to_key(x):
    b = lax.bitcast_convert_type(x, I32)
    return b ^ ((b >> 31) & INT_MAX)


def _fold8(x, op):
    parts = [x[r * SUBLANES:(r + 1) * SUBLANES] for r in range(x.shape[0] // SUBLANES)]
    while len(parts) > 1:
        nxt = [op(parts[a], parts[a + 1]) for a in range(0, len(parts) - 1, 2)]
        if len(parts) % 2:
            nxt.append(parts[-1])
        parts = nxt
    return parts[0]


def _col_reduce8(x8, op):
    x8 = op(x8, pltpu.roll(x8, 4, 0))
    x8 = op(x8, pltpu.roll(x8, 2, 0))
    x8 = op(x8, pltpu.roll(x8, 1, 0))
    return x8[0:1]


_N_K, _N_V, _N_KI, _N_U, _N_VB, _N_GA, _N_GB, _N_END = 0, 256, 512, 640, 1664, 2688, 3712, 4736
_T_Q, _T_V, _T_QI, _T_WI, _T_END = 0, 1024, 1280, 1792, 1808


def _proj_kernel(x_ref, wn_ref, wt_ref, gmix_ref, gqb_ref, gk_ref, gsgu_ref, ws_ref, bs_ref,
                 qt_ref, kf_ref, kb_ref, vf_ref, vt_ref, qi3t_ref, kif_ref, ki3_ref, wit_ref,
                 vn_ref, sga_ref, sgu_ref, *, sample):
    tm = x_ref.shape[0]
    nq = tm // QBLK
    xn = _rms_rows(x_ref[...], gmix_ref[...]).astype(BF16)

    def nat(lo, hi):
        return _dot(xn, wn_ref[:, lo:hi])

    def tra(lo, hi):
        return _dot_t(wt_ref[lo:hi, :], xn)

    z3 = tra(_T_Q, _T_V).reshape(N_HEADS, HEAD_DIM, tm)
    r = lax.rsqrt(jnp.mean(z3 * z3, axis=1, keepdims=True) + EPS)
    qn = (z3 * r).reshape(N_HEADS * HEAD_DIM, tm)
    for n in range(nq):
        qt_ref[n] = (qn[:, n * QBLK:(n + 1) * QBLK] * gqb_ref[...]).astype(BF16)
    vt_ref[...] = tra(_T_V, _T_QI).astype(BF16)

    zqi = tra(_T_QI, _T_WI) * IDX_SCALE
    zero = jnp.zeros((IDX_DIM, tm), BF16)
    for h in range(IDX_HEADS):
        hq, lq = _hi_lo(zqi[h * IDX_DIM:(h + 1) * IDX_DIM])
        blk = jnp.concatenate([hq, lq, hq, zero], axis=0)
        for n in range(nq):
            qi3t_ref[n, :, h * QBLK:(h + 1) * QBLK] = blk[:, n * QBLK:(n + 1) * QBLK]
    zwi = tra(_T_WI, _T_END) * IDX_W_SCALE
    for n in range(nq):
        wit_ref[n] = zwi[0:IDX_HEADS, n * QBLK:(n + 1) * QBLK]

    zk = nat(_N_K, _N_V)
    kn = zk * _head_rms_scale(zk) * gk_ref[...]
    kf_ref[...] = kn
    kb_ref[...] = kn.astype(BF16)
    vf_ref[...] = nat(_N_V, _N_KI)
    first = lax.broadcasted_iota(I32, (tm, LANES), 1) < IDX_DIM
    zki = nat(_N_KI, _N_U)
    kif_ref[...] = zki[:, 0:IDX_DIM]
    hk, lk = _hi_lo(zki)
    ki3_ref[:, 0:LANES] = hk
    ki3_ref[:, LANES:2 * LANES] = jnp.where(first, lk, jnp.zeros_like(lk))

    u = jax.nn.gelu(nat(_N_U, _N_VB))
    vn = _rms_rows(jax.nn.gelu(nat(_N_VB, _N_GA)), gsgu_ref[...])
    vn_ref[...] = vn
    sga_ref[...] = _sigmoid(nat(_N_GA, _N_GB))
    sgb = _sigmoid(nat(_N_GB, _N_END))
    if sample:
        sgu_ref[...] = sgb * (u * (vn * ws_ref[...] + bs_ref[...]))
    else:
        tril = (lax.broadcasted_iota(I32, (CHUNK, CHUNK), 0)
                >= lax.broadcasted_iota(I32, (CHUNK, CHUNK), 1))
        for g in range(SGU_GROUPS):
            wg = jnp.where(tril, ws_ref[g], 0.0).astype(BF16)
            bg = bs_ref[:, g:g + 1]
            for n in range(tm // CHUNK):
                rows = slice(n * CHUNK, (n + 1) * CHUNK)
                cols = slice(g * CHUNK, (g + 1) * CHUNK)
                mixed = _dot(wg, vn[rows, cols].astype(BF16)) + bg
                sgu_ref[rows, cols] = sgb[rows, cols] * (u[rows, cols] * mixed)


def _proj(x, wn, wt, gmix, gqb, gk, gsgu, ws, bs, *, sample, tm):
    t = x.shape[0]
    d = x.shape[1]
    nq = tm // QBLK
    row = lambda i: (i, 0)
    const2 = lambda i: (0, 0)
    blk3 = lambda i: (i, 0, 0)
    if sample:
        ws_spec = pl.BlockSpec(ws.shape, const2)
    else:
        ws_spec = pl.BlockSpec(ws.shape, lambda i: (0, 0, 0))
    bs_spec = pl.BlockSpec(bs.shape, const2)
    out_shapes = (
        jax.ShapeDtypeStruct((t // QBLK, 1024, QBLK), BF16),
        jax.ShapeDtypeStruct((t, 256), F32), jax.ShapeDtypeStruct((t, 256), BF16),
        jax.ShapeDtypeStruct((t, 256), F32), jax.ShapeDtypeStruct((256, t), BF16),
        jax.ShapeDtypeStruct((t // QBLK, 256, IDX_HEADS * QBLK), BF16),
        jax.ShapeDtypeStruct((t, IDX_DIM), F32),
        jax.ShapeDtypeStruct((t, 256), BF16),
        jax.ShapeDtypeStruct((t // QBLK, IDX_HEADS, QBLK), F32),
        jax.ShapeDtypeStruct((t, 1024), F32),
        jax.ShapeDtypeStruct((t, 1024), F32),
        jax.ShapeDtypeStruct((t, 1024), F32),
    )
    out_specs = (
        pl.BlockSpec((nq, 1024, QBLK), blk3),
        pl.BlockSpec((tm, 256), row), pl.BlockSpec((tm, 256), row),
        pl.BlockSpec((tm, 256), row), pl.BlockSpec((256, tm), lambda i: (0, i)),
        pl.BlockSpec((nq, 256, IDX_HEADS * QBLK), blk3),
        pl.BlockSpec((tm, IDX_DIM), row),
        pl.BlockSpec((tm, 256), row),
        pl.BlockSpec((nq, IDX_HEADS, QBLK), blk3),
        pl.BlockSpec((tm, 1024), row),
        pl.BlockSpec((tm, 1024), row),
        pl.BlockSpec((tm, 1024), row),
    )
    return pl.pallas_call(
        functools.partial(_proj_kernel, sample=sample),
        grid=(t // tm,),
        in_specs=[
            pl.BlockSpec((tm, d), row),
            pl.BlockSpec(wn.shape, const2, pipeline_mode=pl.Buffered(1)),
            pl.BlockSpec(wt.shape, const2, pipeline_mode=pl.Buffered(1)),
            pl.BlockSpec((1, d), const2), pl.BlockSpec(gqb.shape, const2),
            pl.BlockSpec((1, 256), const2), pl.BlockSpec((1, 1024), const2),
            ws_spec, bs_spec,
        ],
        out_specs=out_specs,
        out_shape=out_shapes,
        compiler_params=pltpu.CompilerParams(
            dimension_semantics=("arbitrary",), vmem_limit_bytes=VMEM_LIMIT),
        name="proj_sample" if sample else "proj_prompt",
    )(x, wn, wt, gmix, gqb, gk, gsgu, ws, bs)


def _select(key_ref, n_steps, topk, lo0, hi0):
    lanes = key_ref.shape[1]
    topk_f = float(topk)

    def count(pred):
        def body(t, acc):
            start = pl.multiple_of(t * SEL_ROWS, SEL_ROWS)
            tile = key_ref[pl.ds(start, SEL_ROWS), :]
            return acc + _fold8(jnp.where(pred(tile, start), 1.0, 0.0), jnp.add)
        acc = lax.fori_loop(0, n_steps, body, jnp.zeros((SUBLANES, lanes), F32))
        return _col_reduce8(acc, jnp.add)

    def count_ge(thr):
        thr_b = jnp.broadcast_to(thr, (SEL_ROWS, lanes))
        return count(lambda tile, start: tile >= thr_b)

    def cond(st):
        return jnp.logical_and(st[4] > 0.5, st[5] < 40)

    def body(st):
        lo, hi, cnt_lo, cnt_hi, _, it = st
        mid = (lo >> 1) + (hi >> 1) + (lo & hi & 1)
        c = count_ge(mid)
        ge = c >= topk_f
        exact = c == topk_f
        lo2 = jnp.where(ge, mid, lo)
        cnt_lo2 = jnp.where(ge, c, cnt_lo)
        hi2 = jnp.where(exact, mid + 1, jnp.where(ge, hi, mid))
        cnt_hi2 = jnp.where(ge, cnt_hi, c)
        active = jnp.where(hi2 != lo2 + 1, 1.0, 0.0)
        return lo2, hi2, cnt_lo2, cnt_hi2, jnp.max(active), it + 1

    init = (lo0, hi0, jnp.full((1, lanes), 2.0 * topk_f, F32), jnp.zeros((1, lanes), F32),
            jnp.float32(1.0), jnp.int32(0))
    lo, _, cnt_lo, cnt_hi, _, _ = lax.while_loop(cond, body, init)

    tie = jnp.logical_and(cnt_lo > topk_f, lo > INT_MIN)
    need = topk_f - cnt_hi
    any_tie = jnp.max(jnp.where(tie, 1.0, 0.0))

    def tie_search():
        lo_b = jnp.broadcast_to(lo, (SEL_ROWS, lanes))
        row = lax.broadcasted_iota(I32, (SEL_ROWS, lanes), 0)

        def step(_, st):
            jlo, jhi = st
            mid = (jlo + jhi) >> 1
            mid_b = jnp.broadcast_to(mid, (SEL_ROWS, lanes))
            f = count(lambda tile, start: jnp.logical_and(tile == lo_b, row + start <= mid_b))
            ok = f >= need
            return jnp.where(ok, jlo, mid), jnp.where(ok, mid, jhi)

        n_bisect = max(1, (key_ref.shape[0]).bit_length())
        jhi0 = jnp.zeros((1, lanes), I32) + (n_steps * SEL_ROWS - 1)
        _, jhi = lax.fori_loop(0, n_bisect, step, (jnp.full((1, lanes), -1, I32), jhi0))
        return jnp.where(tie, jhi, INT_MAX)

    jcut = lax.cond(any_tie > 0.5, tie_search, lambda: jnp.full((1, lanes), INT_MAX, I32))
    tau = jnp.maximum(lo, INT_MIN + 1)
    return tau, jcut


def _group_bounds(gm):
    lo0 = _col_reduce8(_fold8(gm, jnp.minimum), jnp.minimum)
    hi0 = _col_reduce8(_fold8(gm, jnp.maximum), jnp.maximum) + 1
    return lo0, hi0


def _selected(keys, kpos, tau, jcut):
    return jnp.logical_or(keys > tau, jnp.logical_and(keys == tau, kpos <= jcut))


def _attn_kernel(qt_ref, qi3t_ref, wit_ref, k_ref, vt_ref, ki3_ref, o_ref,
                 key_s, gm_s, wq_s, m_s, l_s, acc_s, *, topk, ck):
    i = pl.program_id(1)
    n_ck = (i * QBLK + QBLK - 1) // ck + 1
    rs = 64
    gq = Q_PER_KV * QBLK

    wq_s[...] = jnp.zeros(wq_s.shape, BF16)
    for c in range(KV_HEADS):
        for g in range(Q_PER_KV):
            h = c * Q_PER_KV + g
            wq_s[c, c * HEAD_DIM:(c + 1) * HEAD_DIM, g * QBLK:(g + 1) * QBLK] = (
                qt_ref[h * HEAD_DIM:(h + 1) * HEAD_DIM, :])
    m_s[...] = jnp.full(m_s.shape, M_INIT, F32)
    l_s[...] = jnp.zeros(l_s.shape, F32)
    acc_s[...] = jnp.zeros(acc_s.shape, F32)
    gm_s[...] = jnp.full(gm_s.shape, INT_MIN, I32)

    qpos = lax.broadcasted_iota(I32, (rs, QBLK), 1) + i * QBLK
    krow = lax.broadcasted_iota(I32, (rs, QBLK), 0)

    def score_chunk(j, carry):
        base = pl.multiple_of(j * ck, ck)
        lg = _dot(ki3_ref[pl.ds(base, ck), :], qi3t_ref[...])
        for p in range(ck // rs):
            acc = jnp.zeros((rs, QBLK), F32)
            for h in range(IDX_HEADS):
                acc = acc + jnp.maximum(lg[p * rs:(p + 1) * rs, h * QBLK:(h + 1) * QBLK], 0.0) * wit_ref[h:h + 1, :]
            key = jnp.where(krow + (base + p * rs) <= qpos, _to_key(acc), INT_MIN)
            key_s[pl.ds(pl.multiple_of(base + p * rs, rs), rs), :] = key
        for p in range(ck // SEL_ROWS):
            gm_s[...] = jnp.maximum(gm_s[...], key_s[pl.ds(pl.multiple_of(base + p * SEL_ROWS, SEL_ROWS), SEL_ROWS), :])
        return carry

    lax.fori_loop(0, n_ck, score_chunk, 0)

    lo0, hi0 = _group_bounds(gm_s[...])
    tau, jcut = _select(key_s, n_ck * (ck // SEL_ROWS), topk, lo0, hi0)

    krow_c = lax.broadcasted_iota(I32, (ck, QBLK), 0)

    def attn_chunk(j, carry):
        base = pl.multiple_of(j * ck, ck)
        kc = k_ref[pl.ds(base, ck), :]
        sel = _selected(key_s[pl.ds(base, ck), :], krow_c + base, tau, jcut)
        bias = jnp.where(sel, 0.0, MASK_BIAS)
        bias = jnp.concatenate([bias] * Q_PER_KV, axis=1)
        for c in range(KV_HEADS):
            s = _dot(kc, wq_s[c]) + bias
            m_old = m_s[c]
            m_new = jnp.maximum(m_old, jnp.max(s, axis=0, keepdims=True))
            p = jnp.exp2(s - m_new)
            alpha = jnp.exp2(m_old - m_new)
            l_s[c] = alpha * l_s[c] + jnp.sum(p, axis=0, keepdims=True)
            m_s[c] = m_new
            vc = vt_ref[c * HEAD_DIM:(c + 1) * HEAD_DIM, pl.ds(base, ck)]
            acc_s[c] = alpha * acc_s[c] + _dot(vc, p.astype(BF16))
        return carry

    lax.fori_loop(0, n_ck, attn_chunk, 0)

    for hp in range(N_HEADS // 2):
        parts = []
        for h in (2 * hp, 2 * hp + 1):
            c, g = divmod(h, Q_PER_KV)
            parts.append(acc_s[c, :, g * QBLK:(g + 1) * QBLK] / l_s[c, :, g * QBLK:(g + 1) * QBLK])
        o_ref[:, hp * LANES:(hp + 1) * LANES] = jnp.concatenate(parts, axis=0).T


def _attn_prompt(qt, qi3t, wit, kb, vt, ki3, *, bsz, seq, topk, ck=512):
    nblk = seq // QBLK
    grid = (bsz, nblk)
    qblk = lambda bi, i: (bi * nblk + i, 0, 0)
    per_seq = lambda bi, i: (bi, 0, 0)
    gq = Q_PER_KV * QBLK
    return pl.pallas_call(
        functools.partial(_attn_kernel, topk=topk, ck=ck),
        grid=grid,
        in_specs=[
            pl.BlockSpec((None, 1024, QBLK), qblk),
            pl.BlockSpec((None, 256, IDX_HEADS * QBLK), qblk),
            pl.BlockSpec((None, IDX_HEADS, QBLK), qblk),
            pl.BlockSpec((None, seq, 256), per_seq),
            pl.BlockSpec((256, seq), lambda bi, i: (0, bi)),
            pl.BlockSpec((None, seq, 256), per_seq),
        ],
        out_specs=pl.BlockSpec((None, QBLK, 1024), lambda bi, i: (bi, i, 0)),
        out_shape=jax.ShapeDtypeStruct((bsz, seq, 1024), F32),
        scratch_shapes=[
            pltpu.VMEM((seq, QBLK), I32),
            pltpu.VMEM((SEL_ROWS, QBLK), I32),
            pltpu.VMEM((KV_HEADS, 256, gq), BF16),
            pltpu.VMEM((KV_HEADS, 1, gq), F32), pltpu.VMEM((KV_HEADS, 1, gq), F32),
            pltpu.VMEM((KV_HEADS, HEAD_DIM, gq), F32),
        ],
        compiler_params=pltpu.CompilerParams(
            dimension_semantics=("arbitrary", "arbitrary"), vmem_limit_bytes=VMEM_LIMIT),
        name="attn_prompt",
    )(qt, qi3t, wit, kb, vt, ki3)


def _dec_score_kernel(pt_ref, qi3_ref, wexp_ref, kis_ref, *rest, n_pages, page):
    page_refs = rest[:n_pages]
    key_ref = rest[n_pages]
    del pt_ref
    n_keys = key_ref.shape[1]
    ki_all = jnp.concatenate(
        [r[...] for r in page_refs]
        + [jnp.broadcast_to(kis_ref[...], (n_keys - n_pages * page, IDX_DIM))], axis=0)
    hk, lk = _hi_lo(ki_all)
    q3 = qi3_ref[...]
    qh = q3[:, 0:IDX_DIM]
    ql = q3[:, IDX_DIM:2 * IDX_DIM]
    lg = _dot_t(qh, hk) + _dot_t(ql, hk) + _dot_t(qh, lk)
    score = jnp.zeros((1, n_keys), F32)
    for h in range(IDX_HEADS):
        score = score + jnp.maximum(lg[h:h + 1, :], 0.0) * wexp_ref[h:h + 1, :]
    kpos = lax.broadcasted_iota(I32, (1, n_keys), 1)
    key_ref[...] = jnp.where(kpos <= n_pages * page, _to_key(score), INT_MIN)


def _dec_select_kernel(key_ref, tau_ref, jcut_ref, *, topk):
    n_steps = key_ref.shape[0] // SEL_ROWS
    gm = key_ref[0:SEL_ROWS, :]
    for t in range(1, n_steps):
        gm = jnp.maximum(gm, key_ref[t * SEL_ROWS:(t + 1) * SEL_ROWS, :])
    lo0, hi0 = _group_bounds(gm)
    tau, jcut = _select(key_ref, n_steps, topk, lo0, hi0)
    tau_ref[...] = tau
    jcut_ref[...] = jcut


def _dec_attn_kernel(pt_ref, tau_ref, jcut_ref, lhs_ref, key_ref, ks_ref, vs_ref, *rest, n_pages, page):
    k_refs = rest[:n_pages]
    v_refs = rest[n_pages:2 * n_pages]
    o_ref = rest[2 * n_pages]
    del pt_ref
    b = pl.program_id(0)
    n_past = n_pages * page
    tau = tau_ref[b]
    jcut = jcut_ref[b]
    lhs = lhs_ref[...]
    keys = key_ref[...]
    kpos = lax.broadcasted_iota(I32, keys.shape, 1)
    bias = jnp.where(_selected(keys, kpos, tau, jcut), 0.0, MASK_BIAS)
    s = jnp.concatenate([_dot_t(lhs, r[...].astype(BF16)) for r in k_refs], axis=1)
    s = s + bias[:, 0:n_past]
    ks = ks_ref[...].astype(BF16).astype(F32)
    s_self = jnp.sum(lhs.astype(F32) * ks, axis=1, keepdims=True) + bias[:, n_past:n_past + 1]
    m = jnp.maximum(jnp.max(s, axis=1, keepdims=True), s_self)
    p = jnp.exp2(s - m)
    p_self = jnp.exp2(s_self - m)
    l = jnp.sum(p, axis=1, keepdims=True) + p_self
    v_all = jnp.concatenate([r[...].astype(BF16) for r in v_refs], axis=0)
    vs = vs_ref[...].astype(BF16).astype(F32)
    o = (_dot(p.astype(BF16), v_all) + p_self.astype(BF16).astype(F32) * vs) / l
    r_i = lax.broadcasted_iota(I32, o.shape, 0)
    l_i = lax.broadcasted_iota(I32, o.shape, 1)
    o = jnp.where((l_i >> HEAD_SHIFT) == (r_i >> 2), o, 0.0)
    o_ref[...] = (o[:, 0:HEAD_DIM] + o[:, HEAD_DIM:2 * HEAD_DIM]
                  + o[:, 2 * HEAD_DIM:3 * HEAD_DIM] + o[:, 3 * HEAD_DIM:4 * HEAD_DIM])


def _attn_decode(page_table, qt_s, qi3t_s, wit_s, kf_s, vf_s, kif_s, cache_k, cache_v, cache_kidx, *, topk):
    n, n_pages = page_table.shape
    n_phys, page = cache_k.shape[0], cache_k.shape[1]
    n_keys = -(-(n_pages * page + 1) // SEL_ROWS) * SEL_ROWS
    ck2 = cache_k.reshape(n_phys, page, KV_HEADS * HEAD_DIM)
    cv2 = cache_v.reshape(n_phys, page, KV_HEADS * HEAD_DIM)

    def page_spec(width, p):
        return pl.BlockSpec((None, page, width), lambda bi, pt, *_: (pt[bi, p], 0, 0))

    per_tok = lambda bi, *_: (bi, 0, 0)
    q_s = jnp.transpose(qt_s[0])
    qi3 = jnp.transpose(qi3t_s[0].reshape(256, IDX_HEADS, n), (2, 1, 0))
    wexp = jnp.broadcast_to(jnp.transpose(wit_s[0])[:, :, None], (n, IDX_HEADS, n_keys))
    qh = q_s.reshape(n, N_HEADS, 1, HEAD_DIM)
    c_of_head = (jnp.arange(N_HEADS) // Q_PER_KV)[None, :, None, None]
    lhs = jnp.where(c_of_head == jnp.arange(KV_HEADS)[None, None, :, None], qh,
                    jnp.zeros_like(qh)).reshape(n, N_HEADS, KV_HEADS * HEAD_DIM)

    keys = pl.pallas_call(
        functools.partial(_dec_score_kernel, n_pages=n_pages, page=page),
        grid_spec=pltpu.PrefetchScalarGridSpec(
            num_scalar_prefetch=1, grid=(n,),
            in_specs=[pl.BlockSpec((None, IDX_HEADS, 256), per_tok),
                      pl.BlockSpec((None, IDX_HEADS, n_keys), per_tok),
                      pl.BlockSpec((None, 1, IDX_DIM), per_tok)]
            + [page_spec(IDX_DIM, p) for p in range(n_pages)],
            out_specs=pl.BlockSpec((None, 1, n_keys), per_tok)),
        out_shape=jax.ShapeDtypeStruct((n, 1, n_keys), I32),
        compiler_params=pltpu.CompilerParams(dimension_semantics=("arbitrary",), vmem_limit_bytes=VMEM_LIMIT),
        name="dec_score",
    )(page_table, qi3, wexp, kif_s.reshape(n, 1, IDX_DIM), *([cache_kidx] * n_pages))

    tau, jcut = pl.pallas_call(
        functools.partial(_dec_select_kernel, topk=topk),
        out_shape=(jax.ShapeDtypeStruct((1, n), I32), jax.ShapeDtypeStruct((1, n), I32)),
        name="dec_select",
    )(jnp.transpose(keys.reshape(n, n_keys)))

    o = pl.pallas_call(
        functools.partial(_dec_attn_kernel, n_pages=n_pages, page=page),
        grid_spec=pltpu.PrefetchScalarGridSpec(
            num_scalar_prefetch=3, grid=(n,),
            in_specs=[pl.BlockSpec((None, N_HEADS, 256), per_tok),
                      pl.BlockSpec((None, 1, n_keys), per_tok),
                      pl.BlockSpec((None, 1, 256), per_tok),
                      pl.BlockSpec((None, 1, 256), per_tok)]
            + [page_spec(256, p) for p in range(n_pages)]
            + [page_spec(256, p) for p in range(n_pages)],
            out_specs=pl.BlockSpec((None, N_HEADS, HEAD_DIM), per_tok)),
        out_shape=jax.ShapeDtypeStruct((n, N_HEADS, HEAD_DIM), F32),
        compiler_params=pltpu.CompilerParams(dimension_semantics=("arbitrary",), vmem_limit_bytes=VMEM_LIMIT),
        name="dec_attn",
    )(page_table, tau[0], jcut[0], lhs, keys, kf_s.reshape(n, 1, 256), vf_s.reshape(n, 1, 256),
      *([ck2] * n_pages), *([cv2] * n_pages))
    return o.reshape(n, N_HEADS * HEAD_DIM)


def _finish_kernel(x_ref, oatt_ref, sga_ref, sgu_ref, p_ref, wo_ref, gffn_ref, wup_ref, wdn_ref,
                   gple_ref, wpg_ref, wp_ref, y_ref):
    merged = (sga_ref[...] * oatt_ref[...] + sgu_ref[...]).astype(BF16)
    x = x_ref[...] + _dot(merged, wo_ref[...])
    hf = _rms_rows(x, gffn_ref[...]).astype(BF16)
    up = jnp.maximum(_dot(hf, wup_ref[...]), 0.0)
    x = x + _dot((up * up).astype(BF16), wdn_ref[...])
    hp = _rms_rows(x, gple_ref[...]).astype(BF16)
    gate = _sigmoid(_dot(hp, wpg_ref[...]))
    y_ref[...] = x + gate * _dot(p_ref[...].astype(BF16), wp_ref[...])


def _finish(x, oatt, sga, sgu, p, wo, gffn, wup, wdn, gple, wpg, wp, *, tm, name):
    t, d = x.shape
    row = lambda i: (i, 0)
    const = lambda i: (0, 0)

    def wspec(w):
        return pl.BlockSpec(w.shape, const, pipeline_mode=pl.Buffered(1))

    return pl.pallas_call(
        _finish_kernel,
        grid=(t // tm,),
        in_specs=[pl.BlockSpec((tm, d), row), pl.BlockSpec((tm, d), row), pl.BlockSpec((tm, d), row),
                  pl.BlockSpec((tm, d), row), pl.BlockSpec((tm, p.shape[1]), row),
                  wspec(wo), pl.BlockSpec((1, d), const), wspec(wup), wspec(wdn),
                  pl.BlockSpec((1, d), const), wspec(wpg), wspec(wp)],
        out_specs=pl.BlockSpec((tm, d), row),
        out_shape=jax.ShapeDtypeStruct((t, d), F32),
        compiler_params=pltpu.CompilerParams(
            dimension_semantics=("arbitrary",), vmem_limit_bytes=VMEM_LIMIT),
        name=name,
    )(x, oatt, sga, sgu, p, wo, gffn, wup, wdn, gple, wpg, wp)


def _pack_w_in(w_in):
    d = w_in.shape[0]
    splits = (1024, 256, 256, 512, 64, 8, 1024, 1024, 1024, 1024)
    offs = [0]
    for n in splits:
        offs.append(offs[-1] + n)
    wq, wk, wv, wqi, wki, wwi, wu, wvb, wga, wgb = [w_in[:, offs[i]:offs[i + 1]] for i in range(10)]
    wn = jnp.concatenate([wk, wv, wki, wki, wu, wvb, wga, wgb], axis=1).astype(BF16)
    wt = jnp.concatenate([wq, wv, wqi, wwi, jnp.zeros((d, _T_END - _T_WI - IDX_HEADS), w_in.dtype)],
                         axis=1).T.astype(BF16)
    return wn, wt


def kernel(x_prompt, x_sample, cache_k, cache_v, cache_kidx, page_table, p_prompt, p_sample, g_mix, w_in, g_q, g_k, g_sgu, w_s, b_s, w_o, g_ffn, w_up, w_down, g_ple, w_pg, w_p):
    depth = w_in.shape[0]
    bsz, seq, d = x_prompt.shape
    n_dec, dec_seq, _ = x_sample.shape
    assert dec_seq == 1 and seq % 512 == 0
    n_pages, page = page_table.shape[1], cache_k.shape[2]
    topk_p = min(TOPK_MAX, seq // 4)
    topk_s = min(TOPK_MAX, (n_pages * page + dec_seq) // 4)
    t_p = bsz * seq

    xp = x_prompt.reshape(t_p, d)
    xs = x_sample.reshape(n_dec, d)
    outs = [[] for _ in range(8)]
    for i in range(depth):
        wn, wt = _pack_w_in(w_in[i])
        gqb = jnp.broadcast_to((jnp.tile(g_q[i], N_HEADS) * (ATTN_SCALE * LOG2E))[:, None], (1024, QBLK))
        gk = jnp.tile(g_k[i], KV_HEADS)[None]
        gmix = g_mix[i][None]
        gsgu = g_sgu[i][None]
        fin_w = (w_o[i].astype(BF16), g_ffn[i][None], w_up[i].astype(BF16), w_down[i].astype(BF16),
                 g_ple[i][None], w_pg[i].astype(BF16), w_p[i].astype(BF16))

        (qt, kf, kb, vf, vt, qi3t, kif, ki3, wit, vn, sga, sgu) = _proj(
            xp, wn, wt, gmix, gqb, gk, gsgu, w_s[i], jnp.transpose(b_s[i]), sample=False, tm=256)
        oatt = _attn_prompt(qt, qi3t, wit, kb.reshape(bsz, seq, 256), vt, ki3.reshape(bsz, seq, 256),
                            bsz=bsz, seq=seq, topk=topk_p)
        xp = _finish(xp, oatt.reshape(t_p, 1024), sga, sgu, p_prompt[i].reshape(t_p, -1), *fin_w,
                     tm=256, name="finish_prompt")
        outs[0].append(kf.reshape(bsz, seq, KV_HEADS, HEAD_DIM))
        outs[1].append(vf.reshape(bsz, seq, KV_HEADS, HEAD_DIM))
        outs[2].append(kif.reshape(bsz, seq, IDX_DIM))
        outs[3].append(vn.reshape(bsz, seq, d))

        w00 = jnp.repeat(w_s[i][:, 0, 0], CHUNK)[None]
        b0 = jnp.repeat(b_s[i][:, 0], CHUNK)[None]
        (qt, kf, kb, vf, vt, qi3t, kif, ki3, wit, vn, sga, sgu) = _proj(
            xs, wn, wt, gmix, gqb, gk, gsgu, w00, b0, sample=True, tm=n_dec)
        oatt = _attn_decode(page_table, qt, qi3t, wit, kf, vf, kif, cache_k[i], cache_v[i], cache_kidx[i],
                            topk=topk_s)
        xs = _finish(xs, oatt, sga, sgu, p_sample[i].reshape(n_dec, -1), *fin_w, tm=n_dec,
                     name="finish_sample")
        outs[4].append(kf.reshape(n_dec, 1, KV_HEADS, HEAD_DIM))
        outs[5].append(vf.reshape(n_dec, 1, KV_HEADS, HEAD_DIM))
        outs[6].append(kif.reshape(n_dec, 1, IDX_DIM))
        outs[7].append(vn.reshape(n_dec, 1, d))

    return (xp.reshape(bsz, seq, d), xs.reshape(n_dec, 1, d), *[jnp.stack(o) for o in outs])
```

```python
import functools
import math

import jax
import jax.numpy as jnp
from jax import lax
from jax.experimental import pallas as pl
from jax.experimental.pallas import tpu as pltpu

F32 = jnp.float32
BF16 = jnp.bfloat16
I32 = jnp.int32

N_HEADS = 16
HEAD_DIM = 64
HEAD_SHIFT = 6
KV_HEADS = 4
Q_PER_KV = N_HEADS // KV_HEADS
IDX_HEADS = 8
IDX_DIM = 64
TOPK_MAX = 256
QBLK = 128
CHUNK = 128
SGU_GROUPS = 8
EPS = 1e-6
LOG2E = math.log2(math.e)
ATTN_SCALE = HEAD_DIM ** -0.5
IDX_SCALE = IDX_DIM ** -0.5
IDX_W_SCALE = IDX_HEADS ** -0.5

LANES = 128
SUBLANES = 8
INT_MIN = -(2 ** 31)
INT_MAX = 2 ** 31 - 1
MASK_BIAS = -1e30
M_INIT = -1e29
VMEM_LIMIT = 56 * 1024 * 1024
SEL_ROWS = 512
SEL_GROUPS = 256
SEL_STEPS_PER_CHECK = 4


def _dot(a, b):
    return jnp.dot(a, b, preferred_element_type=F32)


def _dot_t(a, b):
    return lax.dot_general(a, b, (((1,), (1,)), ((), ())), preferred_element_type=F32)


def _hi_lo(x):
    hi = x.astype(BF16)
    lo = (x - hi.astype(F32)).astype(BF16)
    return hi, lo


def _sigmoid(x):
    return 1.0 / (1.0 + jnp.exp(-x))


def _rms_rows(x, g):
    return x * lax.rsqrt(jnp.mean(x * x, axis=-1, keepdims=True) + EPS) * g


def _head_rms_scale(z):
    w = z.shape[1]
    seg = lax.broadcasted_iota(I32, (w, LANES), 0) >> HEAD_SHIFT
    col = lax.broadcasted_iota(I32, (w, LANES), 1)
    g = (seg == col).astype(BF16)
    row_t = lax.broadcasted_iota(I32, (LANES, w), 0)
    seg_t = lax.broadcasted_iota(I32, (LANES, w), 1) >> HEAD_SHIFT
    g_t = (row_t == seg_t).astype(BF16)
    hi, lo = _hi_lo(z * z)
    ssq = _dot(hi, g) + _dot(lo, g)
    r = lax.rsqrt(ssq * (1.0 / HEAD_DIM) + EPS)
    rh, rl = _hi_lo(r)
    return _dot(rh, g_t) + _dot(rl, g_t)


def _to_key(x):
    b = lax.bitcast_convert_type(x, I32)
    return b ^ ((b >> 31) & INT_MAX)


def _fold8(x, op):
    parts = [x[r * SUBLANES:(r + 1) * SUBLANES] for r in range(x.shape[0] // SUBLANES)]
    while len(parts) > 1:
        nxt = [op(parts[a], parts[a + 1]) for a in range(0, len(parts) - 1, 2)]
        if len(parts) % 2:
            nxt.append(parts[-1])
        parts = nxt
    return parts[0]


def _col_reduce8(x8, op):
    x8 = op(x8, pltpu.roll(x8, 4, 0))
    x8 = op(x8, pltpu.roll(x8, 2, 0))
    x8 = op(x8, pltpu.roll(x8, 1, 0))
    return x8[0:1]


_N_K, _N_V, _N_KI, _N_U, _N_VB, _N_GA, _N_GB, _N_END = 0, 256, 512, 640, 1664, 2688, 3712, 4736
_T_Q, _T_V, _T_QI, _T_WI, _T_END = 0, 1024, 1280, 1792, 1808


def _proj_kernel(x_ref, wn_ref, wt_ref, gmix_ref, gqb_ref, gk_ref, gsgu_ref, ws_ref, bs_ref,
                 qt_ref, kf_ref, kb_ref, vf_ref, vt_ref, qi3t_ref, kif_ref, ki3_ref, wit_ref,
                 vn_ref, sga_ref, sgu_ref, *, sample):
    tm = x_ref.shape[0]
    nq = tm // QBLK
    xn = _rms_rows(x_ref[...], gmix_ref[...]).astype(BF16)

    def nat(lo, hi):
        return _dot(xn, wn_ref[:, lo:hi])

    def tra(lo, hi):
        return _dot_t(wt_ref[lo:hi, :], xn)

    z3 = tra(_T_Q, _T_V).reshape(N_HEADS, HEAD_DIM, tm)
    r = lax.rsqrt(jnp.mean(z3 * z3, axis=1, keepdims=True) + EPS)
    qn = (z3 * r).reshape(N_HEADS * HEAD_DIM, tm)
    for n in range(nq):
        qt_ref[n] = (qn[:, n * QBLK:(n + 1) * QBLK] * gqb_ref[...]).astype(BF16)
    vt_ref[...] = tra(_T_V, _T_QI).astype(BF16)

    zqi = tra(_T_QI, _T_WI) * IDX_SCALE
    zero = jnp.zeros((IDX_DIM, tm), BF16)
    for h in range(IDX_HEADS):
        hq, lq = _hi_lo(zqi[h * IDX_DIM:(h + 1) * IDX_DIM])
        blk = jnp.concatenate([hq, lq, hq, zero], axis=0)
        for n in range(nq):
            qi3t_ref[n, :, h * QBLK:(h + 1) * QBLK] = blk[:, n * QBLK:(n + 1) * QBLK]
    zwi = tra(_T_WI, _T_END) * IDX_W_SCALE
    for n in range(nq):
        wit_ref[n] = zwi[0:IDX_HEADS, n * QBLK:(n + 1) * QBLK]

    zk = nat(_N_K, _N_V)
    kn = zk * _head_rms_scale(zk) * gk_ref[...]
    kf_ref[...] = kn
    kb_ref[...] = kn.astype(BF16)
    vf_ref[...] = nat(_N_V, _N_KI)
    first = lax.broadcasted_iota(I32, (tm, LANES), 1) < IDX_DIM
    zki = nat(_N_KI, _N_U)
    kif_ref[...] = zki[:, 0:IDX_DIM]
    hk, lk = _hi_lo(zki)
    ki3_ref[:, 0:LANES] = hk
    ki3_ref[:, LANES:2 * LANES] = jnp.where(first, lk, jnp.zeros_like(lk))

    u = jax.nn.gelu(nat(_N_U, _N_VB))
    vn = _rms_rows(jax.nn.gelu(nat(_N_VB, _N_GA)), gsgu_ref[...])
    vn_ref[...] = vn
    sga_ref[...] = _sigmoid(nat(_N_GA, _N_GB))
    sgb = _sigmoid(nat(_N_GB, _N_END))
    if sample:
        sgu_ref[...] = sgb * (u * (vn * ws_ref[...] + bs_ref[...]))
    else:
        tril = (lax.broadcasted_iota(I32, (CHUNK, CHUNK), 0)
                >= lax.broadcasted_iota(I32, (CHUNK, CHUNK), 1))
        for g in range(SGU_GROUPS):
            wg = jnp.where(tril, ws_ref[g], 0.0).astype(BF16)
            bg = bs_ref[:, g:g + 1]
            for n in range(tm // CHUNK):
                rows = slice(n * CHUNK, (n + 1) * CHUNK)
                cols = slice(g * CHUNK, (g + 1) * CHUNK)
                mixed = _dot(wg, vn[rows, cols].astype(BF16)) + bg
                sgu_ref[rows, cols] = sgb[rows, cols] * (u[rows, cols] * mixed)


def _proj(x, wn, wt, gmix, gqb, gk, gsgu, ws, bs, *, sample, tm):
    t = x.shape[0]
    d = x.shape[1]
    nq = tm // QBLK
    row = lambda i: (i, 0)
    const2 = lambda i: (0, 0)
    blk3 = lambda i: (i, 0, 0)
    if sample:
        ws_spec = pl.BlockSpec(ws.shape, const2)
    else:
        ws_spec = pl.BlockSpec(ws.shape, lambda i: (0, 0, 0))
    bs_spec = pl.BlockSpec(bs.shape, const2)
    out_shapes = (
        jax.ShapeDtypeStruct((t // QBLK, 1024, QBLK), BF16),
        jax.ShapeDtypeStruct((t, 256), F32), jax.ShapeDtypeStruct((t, 256), BF16),
        jax.ShapeDtypeStruct((t, 256), F32), jax.ShapeDtypeStruct((256, t), BF16),
        jax.ShapeDtypeStruct((t // QBLK, 256, IDX_HEADS * QBLK), BF16),
        jax.ShapeDtypeStruct((t, IDX_DIM), F32),
        jax.ShapeDtypeStruct((t, 256), BF16),
        jax.ShapeDtypeStruct((t // QBLK, IDX_HEADS, QBLK), F32),
        jax.ShapeDtypeStruct((t, 1024), F32),
        jax.ShapeDtypeStruct((t, 1024), F32),
        jax.ShapeDtypeStruct((t, 1024), F32),
    )
    out_specs = (
        pl.BlockSpec((nq, 1024, QBLK), blk3),
        pl.BlockSpec((tm, 256), row), pl.BlockSpec((tm, 256), row),
        pl.BlockSpec((tm, 256), row), pl.BlockSpec((256, tm), lambda i: (0, i)),
        pl.BlockSpec((nq, 256, IDX_HEADS * QBLK), blk3),
        pl.BlockSpec((tm, IDX_DIM), row),
        pl.BlockSpec((tm, 256), row),
        pl.BlockSpec((nq, IDX_HEADS, QBLK), blk3),
        pl.BlockSpec((tm, 1024), row),
        pl.BlockSpec((tm, 1024), row),
        pl.BlockSpec((tm, 1024), row),
    )
    return pl.pallas_call(
        functools.partial(_proj_kernel, sample=sample),
        grid=(t // tm,),
        in_specs=[
            pl.BlockSpec((tm, d), row),
            pl.BlockSpec(wn.shape, const2, pipeline_mode=pl.Buffered(1)),
            pl.BlockSpec(wt.shape, const2, pipeline_mode=pl.Buffered(1)),
            pl.BlockSpec((1, d), const2), pl.BlockSpec(gqb.shape, const2),
            pl.BlockSpec((1, 256), const2), pl.BlockSpec((1, 1024), const2),
            ws_spec, bs_spec,
        ],
        out_specs=out_specs,
        out_shape=out_shapes,
        compiler_params=pltpu.CompilerParams(
            dimension_semantics=("arbitrary",), vmem_limit_bytes=VMEM_LIMIT),
        name="proj_sample" if sample else "proj_prompt",
    )(x, wn, wt, gmix, gqb, gk, gsgu, ws, bs)


def _select(key_ref, n_steps, topk, lo0, hi0):
    lanes = key_ref.shape[1]
    topk_f = float(topk)

    def count(pred):
        def body(t, acc):
            start = pl.multiple_of(t * SEL_ROWS, SEL_ROWS)
            tile = key_ref[pl.ds(start, SEL_ROWS), :]
            return acc + _fold8(jnp.where(pred(tile, start), 1.0, 0.0), jnp.add)
        acc = lax.fori_loop(0, n_steps, body, jnp.zeros((SUBLANES, lanes), F32))
        return _col_reduce8(acc, jnp.add)

    def count_ge(thr):
        thr_b = jnp.broadcast_to(thr, (SEL_ROWS, lanes))
        return count(lambda tile, start: tile >= thr_b)

    def cond(st):
        return jnp.logical_and(st[4] > 0.5, st[5] < 40 // SEL_STEPS_PER_CHECK)

    def step(_, st):
        lo, hi, cnt_lo, cnt_hi = st
        mid = (lo >> 1) + (hi >> 1) + (lo & hi & 1)
        c = count_ge(mid)
        ge = c >= topk_f
        exact = c == topk_f
        return (jnp.where(ge, mid, lo), jnp.where(exact, mid + 1, jnp.where(ge, hi, mid)),
                jnp.where(ge, c, cnt_lo), jnp.where(ge, cnt_hi, c))

    def body(st):
        lo, hi, cnt_lo, cnt_hi = lax.fori_loop(0, SEL_STEPS_PER_CHECK, step, st[:4])
        active = jnp.where(hi != lo + 1, 1.0, 0.0)
        return lo, hi, cnt_lo, cnt_hi, jnp.max(active), st[5] + 1

    init = (lo0, hi0, jnp.full((1, lanes), 2.0 * topk_f, F32), jnp.zeros((1, lanes), F32),
            jnp.float32(1.0), jnp.int32(0))
    lo, _, cnt_lo, cnt_hi, _, _ = lax.while_loop(cond, body, init)

    tie = jnp.logical_and(cnt_lo > topk_f, lo > INT_MIN)
    need = topk_f - cnt_hi
    any_tie = jnp.max(jnp.where(tie, 1.0, 0.0))

    def tie_search():
        lo_b = jnp.broadcast_to(lo, (SEL_ROWS, lanes))
        row = lax.broadcasted_iota(I32, (SEL_ROWS, lanes), 0)

        def step(_, st):
            jlo, jhi = st
            mid = (jlo + jhi) >> 1
            mid_b = jnp.broadcast_to(mid, (SEL_ROWS, lanes))
            f = count(lambda tile, start: jnp.logical_and(tile == lo_b, row + start <= mid_b))
            ok = f >= need
            return jnp.where(ok, jlo, mid), jnp.where(ok, mid, jhi)

        n_bisect = max(1, (key_ref.shape[0]).bit_length())
        jhi0 = jnp.zeros((1, lanes), I32) + (n_steps * SEL_ROWS - 1)
        _, jhi = lax.fori_loop(0, n_bisect, step, (jnp.full((1, lanes), -1, I32), jhi0))
        return jnp.where(tie, jhi, INT_MAX)

    jcut = lax.cond(any_tie > 0.5, tie_search, lambda: jnp.full((1, lanes), INT_MAX, I32))
    tau = jnp.maximum(lo, INT_MIN + 1)
    return tau, jcut


def _group_bounds(gm):
    lo0 = _col_reduce8(_fold8(gm, jnp.minimum), jnp.minimum)
    hi0 = _col_reduce8(_fold8(gm, jnp.maximum), jnp.maximum) + 1
    return lo0, hi0


def _selected(keys, kpos, tau, jcut):
    return jnp.logical_or(keys > tau, jnp.logical_and(keys == tau, kpos <= jcut))


def _attn_kernel(qt_ref, qi3t_ref, wit_ref, k_ref, vt_ref, ki3_ref, o_ref,
                 key_s, gm_s, wq_s, m_s, l_s, acc_s, sa_s, sb_s, *, topk, ck):
    i = pl.program_id(1)
    n_ck = (i * QBLK + QBLK - 1) // ck + 1
    rs = 64
    gq = Q_PER_KV * QBLK

    wq_s[...] = jnp.zeros(wq_s.shape, BF16)
    for c in range(KV_HEADS):
        for g in range(Q_PER_KV):
            h = c * Q_PER_KV + g
            wq_s[c, c * HEAD_DIM:(c + 1) * HEAD_DIM, g * QBLK:(g + 1) * QBLK] = (
                qt_ref[h * HEAD_DIM:(h + 1) * HEAD_DIM, :])
    m_s[...] = jnp.full(m_s.shape, M_INIT, F32)
    l_s[...] = jnp.zeros(l_s.shape, F32)
    acc_s[...] = jnp.zeros(acc_s.shape, F32)
    gm_s[...] = jnp.full(gm_s.shape, INT_MIN, I32)

    qpos = lax.broadcasted_iota(I32, (rs, QBLK), 1) + i * QBLK
    krow = lax.broadcasted_iota(I32, (rs, QBLK), 0)

    def score_chunk(j, carry):
        base = pl.multiple_of(j * ck, ck)
        lg = _dot(ki3_ref[pl.ds(base, ck), :], qi3t_ref[...])
        for p in range(ck // rs):
            acc = jnp.zeros((rs, QBLK), F32)
            for h in range(IDX_HEADS):
                acc = acc + jnp.maximum(lg[p * rs:(p + 1) * rs, h * QBLK:(h + 1) * QBLK], 0.0) * wit_ref[h:h + 1, :]
            key = jnp.where(krow + (base + p * rs) <= qpos, _to_key(acc), INT_MIN)
            key_s[pl.ds(pl.multiple_of(base + p * rs, rs), rs), :] = key
        for p in range(ck // SEL_GROUPS):
            start = pl.multiple_of(base + p * SEL_GROUPS, SEL_GROUPS)
            gm_s[...] = jnp.maximum(gm_s[...], key_s[pl.ds(start, SEL_GROUPS), :])
        return carry

    lax.fori_loop(0, n_ck, score_chunk, 0)

    lo0, hi0 = _group_bounds(gm_s[...])
    tau, jcut = _select(key_s, n_ck * (ck // SEL_ROWS), topk, lo0, hi0)

    krow_c = lax.broadcasted_iota(I32, (ck, QBLK), 0)
    s_bufs = (sa_s, sb_s)
    last_ck = k_ref.shape[0] // ck - 1

    def qk(j, c, buf):
        base = pl.multiple_of(j * ck, ck)
        buf[...] = _dot(k_ref[pl.ds(base, ck), :], wq_s[c])

    def softmax_pv(base, c, buf, bias):
        s = buf[...] + bias
        m_old = m_s[c]
        m_new = jnp.maximum(m_old, jnp.max(s, axis=0, keepdims=True))
        p = jnp.exp2(s - m_new)
        alpha = jnp.exp2(m_old - m_new)
        l_s[c] = alpha * l_s[c] + jnp.sum(p, axis=0, keepdims=True)
        m_s[c] = m_new
        vc = vt_ref[c * HEAD_DIM:(c + 1) * HEAD_DIM, pl.ds(base, ck)]
        acc_s[c] = alpha * acc_s[c] + _dot(vc, p.astype(BF16))

    qk(0, 0, s_bufs[0])

    def attn_chunk(j, carry):
        base = pl.multiple_of(j * ck, ck)
        sel = _selected(key_s[pl.ds(base, ck), :], krow_c + base, tau, jcut)
        bias = jnp.where(sel, 0.0, MASK_BIAS)
        bias = jnp.concatenate([bias] * Q_PER_KV, axis=1)
        for c in range(KV_HEADS):
            if c + 1 < KV_HEADS:
                qk(j, c + 1, s_bufs[(c + 1) % 2])
            else:
                qk(jnp.minimum(j + 1, last_ck), 0, s_bufs[0])
            softmax_pv(base, c, s_bufs[c % 2], bias)
        return carry

    lax.fori_loop(0, n_ck, attn_chunk, 0)

    for hp in range(N_HEADS // 2):
        parts = []
        for h in (2 * hp, 2 * hp + 1):
            c, g = divmod(h, Q_PER_KV)
            parts.append(acc_s[c, :, g * QBLK:(g + 1) * QBLK] / l_s[c, :, g * QBLK:(g + 1) * QBLK])
        o_ref[:, hp * LANES:(hp + 1) * LANES] = jnp.concatenate(parts, axis=0).T


def _attn_prompt(qt, qi3t, wit, kb, vt, ki3, *, bsz, seq, topk, ck=512):
    nblk = seq // QBLK
    grid = (bsz, nblk)
    qblk = lambda bi, i: (bi * nblk + i, 0, 0)
    per_seq = lambda bi, i: (bi, 0, 0)
    gq = Q_PER_KV * QBLK
    return pl.pallas_call(
        functools.partial(_attn_kernel, topk=topk, ck=ck),
        grid=grid,
        in_specs=[
            pl.BlockSpec((None, 1024, QBLK), qblk),
            pl.BlockSpec((None, 256, IDX_HEADS * QBLK), qblk),
            pl.BlockSpec((None, IDX_HEADS, QBLK), qblk),
            pl.BlockSpec((None, seq, 256), per_seq, pipeline_mode=pl.Buffered(1)),
            pl.BlockSpec((256, seq), lambda bi, i: (0, bi), pipeline_mode=pl.Buffered(1)),
            pl.BlockSpec((None, seq, 256), per_seq, pipeline_mode=pl.Buffered(1)),
        ],
        out_specs=pl.BlockSpec((None, QBLK, 1024), lambda bi, i: (bi, i, 0)),
        out_shape=jax.ShapeDtypeStruct((bsz, seq, 1024), F32),
        scratch_shapes=[
            pltpu.VMEM((seq, QBLK), I32),
            pltpu.VMEM((SEL_GROUPS, QBLK), I32),
            pltpu.VMEM((KV_HEADS, 256, gq), BF16),
            pltpu.VMEM((KV_HEADS, 1, gq), F32), pltpu.VMEM((KV_HEADS, 1, gq), F32),
            pltpu.VMEM((KV_HEADS, HEAD_DIM, gq), F32),
            pltpu.VMEM((ck, gq), F32), pltpu.VMEM((ck, gq), F32),
        ],
        compiler_params=pltpu.CompilerParams(
            dimension_semantics=("arbitrary", "arbitrary"), vmem_limit_bytes=VMEM_LIMIT),
        name="attn_prompt",
    )(qt, qi3t, wit, kb, vt, ki3)


def _dec_score_kernel(pt_ref, qi3_ref, wcol_ref, kis_ref, *rest, n_pages, page):
    page_refs = rest[:n_pages]
    key_ref = rest[n_pages]
    del pt_ref
    n_keys = key_ref.shape[1]
    n_past = n_pages * page
    ki_all = jnp.concatenate(
        [r[...] for r in page_refs] + [jnp.broadcast_to(kis_ref[...], (IDX_DIM, n_keys - n_past))], axis=1)
    hk, lk = _hi_lo(ki_all)
    q3 = qi3_ref[...]
    qh = q3[:, 0:IDX_DIM]
    ql = q3[:, IDX_DIM:2 * IDX_DIM]
    lg = _dot(qh, hk) + _dot(ql, hk) + _dot(qh, lk)
    score = jnp.sum(jnp.maximum(lg, 0.0) * wcol_ref[...], axis=0, keepdims=True) + 0.0
    kpos = lax.broadcasted_iota(I32, (1, n_keys), 1)
    key_ref[...] = jnp.where(kpos <= n_past, _to_key(score), INT_MIN)


def _dec_select_kernel(key_ref, tau_ref, jcut_ref, *, topk):
    gm = key_ref[0:SEL_GROUPS, :]
    for t in range(1, key_ref.shape[0] // SEL_GROUPS):
        gm = jnp.maximum(gm, key_ref[t * SEL_GROUPS:(t + 1) * SEL_GROUPS, :])
    lo0, hi0 = _group_bounds(gm)
    tau, jcut = _select(key_ref, key_ref.shape[0] // SEL_ROWS, topk, lo0, hi0)
    tau_ref[...] = tau
    jcut_ref[...] = jcut


def _dec_attn_kernel(pt_ref, tau_ref, jcut_ref, lhs_ref, key_ref, ks_ref, vs_ref, *rest, n_pages, page):
    k_refs = rest[:n_pages]
    v_refs = rest[n_pages:2 * n_pages]
    o_ref = rest[2 * n_pages]
    del pt_ref
    b = pl.program_id(0)
    n_past = n_pages * page
    tau = tau_ref[b]
    jcut = jcut_ref[b]
    lhs = lhs_ref[...]
    keys = key_ref[...]
    kpos = lax.broadcasted_iota(I32, keys.shape, 1)
    bias = jnp.where(_selected(keys, kpos, tau, jcut), 0.0, MASK_BIAS)
    s = jnp.concatenate([_dot(lhs, r[...].astype(BF16)) for r in k_refs], axis=1)
    s = s + bias[:, 0:n_past]
    ks = ks_ref[...].astype(BF16).astype(F32)
    s_self = jnp.sum(lhs.astype(F32) * ks, axis=1, keepdims=True) + bias[:, n_past:n_past + 1]
    m = jnp.maximum(jnp.max(s, axis=1, keepdims=True), s_self)
    p = jnp.exp2(s - m)
    p_self = jnp.exp2(s_self - m)
    l = jnp.sum(p, axis=1, keepdims=True) + p_self
    vt_all = jnp.concatenate([r[...].astype(BF16) for r in v_refs], axis=1)
    vs = vs_ref[...].astype(BF16).astype(F32)
    o = (_dot_t(p.astype(BF16), vt_all) + p_self.astype(BF16).astype(F32) * vs) / l
    r_i = lax.broadcasted_iota(I32, o.shape, 0)
    l_i = lax.broadcasted_iota(I32, o.shape, 1)
    o = jnp.where((l_i >> HEAD_SHIFT) == (r_i >> 2), o, 0.0)
    o_ref[...] = (o[:, 0:HEAD_DIM] + o[:, HEAD_DIM:2 * HEAD_DIM]
                  + o[:, 2 * HEAD_DIM:3 * HEAD_DIM] + o[:, 3 * HEAD_DIM:4 * HEAD_DIM])


def _attn_decode(page_table, qt_s, qi3t_s, wit_s, kf_s, vf_s, kif_s, cache_k, cache_v, cache_kidx, *, topk):
    n, n_pages = page_table.shape
    n_phys, page = cache_k.shape[0], cache_k.shape[1]
    n_keys = -(-(n_pages * page + 1) // SEL_ROWS) * SEL_ROWS
    ckt = jnp.transpose(cache_k, (0, 2, 3, 1)).reshape(n_phys, KV_HEADS * HEAD_DIM, page)
    cvt = jnp.transpose(cache_v, (0, 2, 3, 1)).reshape(n_phys, KV_HEADS * HEAD_DIM, page)
    ckit = jnp.transpose(cache_kidx, (0, 2, 1))

    def page_spec(rows, p):
        return pl.BlockSpec((None, rows, page), lambda bi, pt, *_: (pt[bi, p], 0, 0))

    per_tok = lambda bi, *_: (bi, 0, 0)
    q_s = jnp.transpose(qt_s[0])
    qi3 = jnp.transpose(qi3t_s[0].reshape(256, IDX_HEADS, n), (2, 1, 0))
    wcol = jnp.transpose(wit_s[0])[:, :, None]
    qh = q_s.reshape(n, N_HEADS, 1, HEAD_DIM)
    c_of_head = (jnp.arange(N_HEADS) // Q_PER_KV)[None, :, None, None]
    lhs = jnp.where(c_of_head == jnp.arange(KV_HEADS)[None, None, :, None], qh,
                    jnp.zeros_like(qh)).reshape(n, N_HEADS, KV_HEADS * HEAD_DIM)

    keys = pl.pallas_call(
        functools.partial(_dec_score_kernel, n_pages=n_pages, page=page),
        grid_spec=pltpu.PrefetchScalarGridSpec(
            num_scalar_prefetch=1, grid=(n,),
            in_specs=[pl.BlockSpec((None, IDX_HEADS, 256), per_tok),
                      pl.BlockSpec((None, IDX_HEADS, 1), per_tok),
                      pl.BlockSpec((None, IDX_DIM, 1), per_tok)]
            + [page_spec(IDX_DIM, p) for p in range(n_pages)],
            out_specs=pl.BlockSpec((None, 1, n_keys), per_tok)),
        out_shape=jax.ShapeDtypeStruct((n, 1, n_keys), I32),
        compiler_params=pltpu.CompilerParams(dimension_semantics=("arbitrary",), vmem_limit_bytes=VMEM_LIMIT),
        name="dec_score",
    )(page_table, qi3, wcol, kif_s.reshape(n, IDX_DIM, 1), *([ckit] * n_pages))

    tau, jcut = pl.pallas_call(
        functools.partial(_dec_select_kernel, topk=topk),
        out_shape=(jax.ShapeDtypeStruct((1, n), I32), jax.ShapeDtypeStruct((1, n), I32)),
        name="dec_select",
    )(jnp.transpose(keys.reshape(n, n_keys)))

    o = pl.pallas_call(
        functools.partial(_dec_attn_kernel, n_pages=n_pages, page=page),
        grid_spec=pltpu.PrefetchScalarGridSpec(
            num_scalar_prefetch=3, grid=(n,),
            in_specs=[pl.BlockSpec((None, N_HEADS, 256), per_tok),
                      pl.BlockSpec((None, 1, n_keys), per_tok),
                      pl.BlockSpec((None, 1, 256), per_tok),
                      pl.BlockSpec((None, 1, 256), per_tok)]
            + [page_spec(KV_HEADS * HEAD_DIM, p) for p in range(n_pages)]
            + [page_spec(KV_HEADS * HEAD_DIM, p) for p in range(n_pages)],
            out_specs=pl.BlockSpec((None, N_HEADS, HEAD_DIM), per_tok)),
        out_shape=jax.ShapeDtypeStruct((n, N_HEADS, HEAD_DIM), F32),
        compiler_params=pltpu.CompilerParams(dimension_semantics=("arbitrary",), vmem_limit_bytes=VMEM_LIMIT),
        name="dec_attn",
    )(page_table, tau[0], jcut[0], lhs, keys, kf_s.reshape(n, 1, 256), vf_s.reshape(n, 1, 256),
      *([ckt] * n_pages), *([cvt] * n_pages))
    return o.reshape(n, N_HEADS * HEAD_DIM)


def _finish_kernel(x_ref, oatt_ref, sga_ref, sgu_ref, p_ref, wo_ref, gffn_ref, wup_ref, wdn_ref,
                   gple_ref, wpg_ref, wp_ref, y_ref):
    merged = (sga_ref[...] * oatt_ref[...] + sgu_ref[...]).astype(BF16)
    x = x_ref[...] + _dot(merged, wo_ref[...])
    hf = _rms_rows(x, gffn_ref[...]).astype(BF16)
    up = jnp.maximum(_dot(hf, wup_ref[...]), 0.0)
    x = x + _dot((up * up).astype(BF16), wdn_ref[...])
    hp = _rms_rows(x, gple_ref[...]).astype(BF16)
    gate = _sigmoid(_dot(hp, wpg_ref[...]))
    y_ref[...] = x + gate * _dot(p_ref[...].astype(BF16), wp_ref[...])


def _finish(x, oatt, sga, sgu, p, wo, gffn, wup, wdn, gple, wpg, wp, *, tm, name):
    t, d = x.shape
    row = lambda i: (i, 0)
    const = lambda i: (0, 0)

    def wspec(w):
        return pl.BlockSpec(w.shape, const, pipeline_mode=pl.Buffered(1))

    return pl.pallas_call(
        _finish_kernel,
        grid=(t // tm,),
        in_specs=[pl.BlockSpec((tm, d), row), pl.BlockSpec((tm, d), row), pl.BlockSpec((tm, d), row),
                  pl.BlockSpec((tm, d), row), pl.BlockSpec((tm, p.shape[1]), row),
                  wspec(wo), pl.BlockSpec((1, d), const), wspec(wup), wspec(wdn),
                  pl.BlockSpec((1, d), const), wspec(wpg), wspec(wp)],
        out_specs=pl.BlockSpec((tm, d), row),
        out_shape=jax.ShapeDtypeStruct((t, d), F32),
        compiler_params=pltpu.CompilerParams(
            dimension_semantics=("arbitrary",), vmem_limit_bytes=VMEM_LIMIT),
        name=name,
    )(x, oatt, sga, sgu, p, wo, gffn, wup, wdn, gple, wpg, wp)


def _pack_w_in(w_in):
    d = w_in.shape[0]
    splits = (1024, 256, 256, 512, 64, 8, 1024, 1024, 1024, 1024)
    offs = [0]
    for n in splits:
        offs.append(offs[-1] + n)
    wq, wk, wv, wqi, wki, wwi, wu, wvb, wga, wgb = [w_in[:, offs[i]:offs[i + 1]] for i in range(10)]
    wn = jnp.concatenate([wk, wv, wki, wki, wu, wvb, wga, wgb], axis=1).astype(BF16)
    wt = jnp.concatenate([wq, wv, wqi, wwi, jnp.zeros((d, _T_END - _T_WI - IDX_HEADS), w_in.dtype)],
                         axis=1).T.astype(BF16)
    return wn, wt


def kernel(x_prompt, x_sample, cache_k, cache_v, cache_kidx, page_table, p_prompt, p_sample, g_mix, w_in, g_q, g_k, g_sgu, w_s, b_s, w_o, g_ffn, w_up, w_down, g_ple, w_pg, w_p):
    depth = w_in.shape[0]
    bsz, seq, d = x_prompt.shape
    n_dec, dec_seq, _ = x_sample.shape
    assert dec_seq == 1 and seq % 512 == 0
    n_pages, page = page_table.shape[1], cache_k.shape[2]
    topk_p = min(TOPK_MAX, seq // 4)
    topk_s = min(TOPK_MAX, (n_pages * page + dec_seq) // 4)
    t_p = bsz * seq

    xp = x_prompt.reshape(t_p, d)
    xs = x_sample.reshape(n_dec, d)
    outs = [[] for _ in range(8)]
    for i in range(depth):
        wn, wt = _pack_w_in(w_in[i])
        gqb = jnp.broadcast_to((jnp.tile(g_q[i], N_HEADS) * (ATTN_SCALE * LOG2E))[:, None], (1024, QBLK))
        gk = jnp.tile(g_k[i], KV_HEADS)[None]
        gmix = g_mix[i][None]
        gsgu = g_sgu[i][None]
        fin_w = (w_o[i].astype(BF16), g_ffn[i][None], w_up[i].astype(BF16), w_down[i].astype(BF16),
                 g_ple[i][None], w_pg[i].astype(BF16), w_p[i].astype(BF16))

        (qt, kf, kb, vf, vt, qi3t, kif, ki3, wit, vn, sga, sgu) = _proj(
            xp, wn, wt, gmix, gqb, gk, gsgu, w_s[i], jnp.transpose(b_s[i]), sample=False, tm=256)
        oatt = _attn_prompt(qt, qi3t, wit, kb.reshape(bsz, seq, 256), vt, ki3.reshape(bsz, seq, 256),
                            bsz=bsz, seq=seq, topk=topk_p)
        xp = _finish(xp, oatt.reshape(t_p, 1024), sga, sgu, p_prompt[i].reshape(t_p, -1), *fin_w,
                     tm=256, name="finish_prompt")
        outs[0].append(kf.reshape(bsz, seq, KV_HEADS, HEAD_DIM))
        outs[1].append(vf.reshape(bsz, seq, KV_HEADS, HEAD_DIM))
        outs[2].append(kif.reshape(bsz, seq, IDX_DIM))
        outs[3].append(vn.reshape(bsz, seq, d))

        w00 = jnp.repeat(w_s[i][:, 0, 0], CHUNK)[None]
        b0 = jnp.repeat(b_s[i][:, 0], CHUNK)[None]
        (qt, kf, kb, vf, vt, qi3t, kif, ki3, wit, vn, sga, sgu) = _proj(
            xs, wn, wt, gmix, gqb, gk, gsgu, w00, b0, sample=True, tm=n_dec)
        oatt = _attn_decode(page_table, qt, qi3t, wit, kf, vf, kif, cache_k[i], cache_v[i], cache_kidx[i],
                            topk=topk_s)
        xs = _finish(xs, oatt, sga, sgu, p_sample[i].reshape(n_dec, -1), *fin_w, tm=n_dec,
                     name="finish_sample")
        outs[4].append(kf.reshape(n_dec, 1, KV_HEADS, HEAD_DIM))
        outs[5].append(vf.reshape(n_dec, 1, KV_HEADS, HEAD_DIM))
        outs[6].append(kif.reshape(n_dec, 1, IDX_DIM))
        outs[7].append(vn.reshape(n_dec, 1, d))

    return (xp.reshape(bsz, seq, d), xs.reshape(n_dec, 1, d), *[jnp.stack(o) for o in outs])
```

```python
import functools
import math

import jax
import jax.numpy as jnp
from jax import lax
from jax.experimental import pallas as pl
from jax.experimental.pallas import tpu as pltpu

F32 = jnp.float32
BF16 = jnp.bfloat16
I32 = jnp.int32

N_HEADS = 16
HEAD_DIM = 64
HEAD_SHIFT = 6
KV_HEADS = 4
Q_PER_KV = N_HEADS // KV_HEADS
IDX_HEADS = 8
IDX_DIM = 64
TOPK_MAX = 256
QBLK = 128
CHUNK = 128
SGU_GROUPS = 8
EPS = 1e-6
LOG2E = math.log2(math.e)
ATTN_SCALE = HEAD_DIM ** -0.5
IDX_SCALE = IDX_DIM ** -0.5
IDX_W_SCALE = IDX_HEADS ** -0.5

LANES = 128
SUBLANES = 8
INT_MAX = 2 ** 31 - 1
KEY_NEG_INF = -(2 ** 31) + 0x7FFFFF
KEY_POS_INF = 0x7F800000
KEY_MIN_NORMAL = 0x00800000
F32_LOWEST = -3.4028234663852886e38
MASK_BIAS = -1e30
M_INIT = -1e29
VMEM_LIMIT = 56 * 1024 * 1024
SEL_ROWS = 512
SEL_GROUPS = 256
SEL_STEPS_PER_CHECK = 4
ONES_ROWS = 16


def _dot(a, b):
    return jnp.dot(a, b, preferred_element_type=F32)


def _dot_t(a, b):
    return lax.dot_general(a, b, (((1,), (1,)), ((), ())), preferred_element_type=F32)


def _hi_lo(x):
    hi = x.astype(BF16)
    lo = (x - hi.astype(F32)).astype(BF16)
    return hi, lo


def _sigmoid(x):
    return 1.0 / (1.0 + jnp.exp(-x))


def _rms_rows(x, g):
    return x * lax.rsqrt(jnp.mean(x * x, axis=-1, keepdims=True) + EPS) * g


def _head_rms_scale(z):
    w = z.shape[1]
    seg = lax.broadcasted_iota(I32, (w, LANES), 0) >> HEAD_SHIFT
    col = lax.broadcasted_iota(I32, (w, LANES), 1)
    g = (seg == col).astype(BF16)
    row_t = lax.broadcasted_iota(I32, (LANES, w), 0)
    seg_t = lax.broadcasted_iota(I32, (LANES, w), 1) >> HEAD_SHIFT
    g_t = (row_t == seg_t).astype(BF16)
    hi, lo = _hi_lo(z * z)
    ssq = _dot(hi, g) + _dot(lo, g)
    r = lax.rsqrt(ssq * (1.0 / HEAD_DIM) + EPS)
    rh, rl = _hi_lo(r)
    return _dot(rh, g_t) + _dot(rl, g_t)


def _to_key(x):
    b = lax.bitcast_convert_type(x, I32)
    return b ^ ((b >> 31) & INT_MAX)


def _fold8(x, op):
    parts = [x[r * SUBLANES:(r + 1) * SUBLANES] for r in range(x.shape[0] // SUBLANES)]
    while len(parts) > 1:
        nxt = [op(parts[a], parts[a + 1]) for a in range(0, len(parts) - 1, 2)]
        if len(parts) % 2:
            nxt.append(parts[-1])
        parts = nxt
    return parts[0]


def _col_reduce8(x8, op):
    x8 = op(x8, pltpu.roll(x8, 4, 0))
    x8 = op(x8, pltpu.roll(x8, 2, 0))
    x8 = op(x8, pltpu.roll(x8, 1, 0))
    return x8[0:1]


_N_K, _N_V, _N_KI, _N_U, _N_VB, _N_GA, _N_GB, _N_END = 0, 256, 512, 640, 1664, 2688, 3712, 4736
_T_Q, _T_V, _T_QI, _T_WI, _T_END = 0, 1024, 1280, 1792, 1808


def _proj_kernel(x_ref, wn_ref, wt_ref, gmix_ref, gqb_ref, gk_ref, gsgu_ref, ws_ref, bs_ref,
                 qt_ref, kf_ref, kb_ref, vf_ref, vt_ref, qi3t_ref, kif_ref, ki3_ref, wit_ref,
                 vn_ref, sga_ref, sgu_ref, *, sample):
    tm = x_ref.shape[0]
    nq = tm // QBLK
    xn = _rms_rows(x_ref[...], gmix_ref[...]).astype(BF16)

    def nat(lo, hi):
        return _dot(xn, wn_ref[:, lo:hi])

    def tra(lo, hi):
        return _dot_t(wt_ref[lo:hi, :], xn)

    z3 = tra(_T_Q, _T_V).reshape(N_HEADS, HEAD_DIM, tm)
    r = lax.rsqrt(jnp.mean(z3 * z3, axis=1, keepdims=True) + EPS)
    qn = (z3 * r).reshape(N_HEADS * HEAD_DIM, tm)
    for n in range(nq):
        qt_ref[n] = (qn[:, n * QBLK:(n + 1) * QBLK] * gqb_ref[...]).astype(BF16)
    vt_ref[...] = tra(_T_V, _T_QI).astype(BF16)

    zqi = tra(_T_QI, _T_WI) * IDX_SCALE
    zero = jnp.zeros((IDX_DIM, tm), BF16)
    for h in range(IDX_HEADS):
        hq, lq = _hi_lo(zqi[h * IDX_DIM:(h + 1) * IDX_DIM])
        blk = jnp.concatenate([hq, lq, hq, zero], axis=0)
        for n in range(nq):
            qi3t_ref[n, :, h * QBLK:(h + 1) * QBLK] = blk[:, n * QBLK:(n + 1) * QBLK]
    zwi = tra(_T_WI, _T_END) * IDX_W_SCALE
    for n in range(nq):
        wit_ref[n] = zwi[0:IDX_HEADS, n * QBLK:(n + 1) * QBLK]

    zk = nat(_N_K, _N_V)
    kn = zk * _head_rms_scale(zk) * gk_ref[...]
    kf_ref[...] = kn
    kb_ref[...] = kn.astype(BF16)
    vf_ref[...] = nat(_N_V, _N_KI)
    first = lax.broadcasted_iota(I32, (tm, LANES), 1) < IDX_DIM
    zki = nat(_N_KI, _N_U)
    kif_ref[...] = zki[:, 0:IDX_DIM]
    hk, lk = _hi_lo(zki)
    ki3_ref[:, 0:LANES] = hk
    ki3_ref[:, LANES:2 * LANES] = jnp.where(first, lk, jnp.zeros_like(lk))

    u = jax.nn.gelu(nat(_N_U, _N_VB))
    vn = _rms_rows(jax.nn.gelu(nat(_N_VB, _N_GA)), gsgu_ref[...])
    vn_ref[...] = vn
    sga_ref[...] = _sigmoid(nat(_N_GA, _N_GB))
    sgb = _sigmoid(nat(_N_GB, _N_END))
    if sample:
        sgu_ref[...] = sgb * (u * (vn * ws_ref[...] + bs_ref[...]))
    else:
        tril = (lax.broadcasted_iota(I32, (CHUNK, CHUNK), 0)
                >= lax.broadcasted_iota(I32, (CHUNK, CHUNK), 1))
        for g in range(SGU_GROUPS):
            wg = jnp.where(tril, ws_ref[g], 0.0).astype(BF16)
            bg = bs_ref[:, g:g + 1]
            for n in range(tm // CHUNK):
                rows = slice(n * CHUNK, (n + 1) * CHUNK)
                cols = slice(g * CHUNK, (g + 1) * CHUNK)
                mixed = _dot(wg, vn[rows, cols].astype(BF16)) + bg
                sgu_ref[rows, cols] = sgb[rows, cols] * (u[rows, cols] * mixed)


def _proj(x, wn, wt, gmix, gqb, gk, gsgu, ws, bs, *, sample, tm):
    t = x.shape[0]
    d = x.shape[1]
    nq = tm // QBLK
    row = lambda i: (i, 0)
    const2 = lambda i: (0, 0)
    blk3 = lambda i: (i, 0, 0)
    if sample:
        ws_spec = pl.BlockSpec(ws.shape, const2)
    else:
        ws_spec = pl.BlockSpec(ws.shape, lambda i: (0, 0, 0))
    bs_spec = pl.BlockSpec(bs.shape, const2)
    out_shapes = (
        jax.ShapeDtypeStruct((t // QBLK, 1024, QBLK), BF16),
        jax.ShapeDtypeStruct((t, 256), F32), jax.ShapeDtypeStruct((t, 256), BF16),
        jax.ShapeDtypeStruct((t, 256), F32), jax.ShapeDtypeStruct((256, t), BF16),
        jax.ShapeDtypeStruct((t // QBLK, 256, IDX_HEADS * QBLK), BF16),
        jax.ShapeDtypeStruct((t, IDX_DIM), F32),
        jax.ShapeDtypeStruct((t, 256), BF16),
        jax.ShapeDtypeStruct((t // QBLK, IDX_HEADS, QBLK), F32),
        jax.ShapeDtypeStruct((t, 1024), F32),
        jax.ShapeDtypeStruct((t, 1024), F32),
        jax.ShapeDtypeStruct((t, 1024), F32),
    )
    out_specs = (
        pl.BlockSpec((nq, 1024, QBLK), blk3),
        pl.BlockSpec((tm, 256), row), pl.BlockSpec((tm, 256), row),
        pl.BlockSpec((tm, 256), row), pl.BlockSpec((256, tm), lambda i: (0, i)),
        pl.BlockSpec((nq, 256, IDX_HEADS * QBLK), blk3),
        pl.BlockSpec((tm, IDX_DIM), row),
        pl.BlockSpec((tm, 256), row),
        pl.BlockSpec((nq, IDX_HEADS, QBLK), blk3),
        pl.BlockSpec((tm, 1024), row),
        pl.BlockSpec((tm, 1024), row),
        pl.BlockSpec((tm, 1024), row),
    )
    return pl.pallas_call(
        functools.partial(_proj_kernel, sample=sample),
        grid=(t // tm,),
        in_specs=[
            pl.BlockSpec((tm, d), row),
            pl.BlockSpec(wn.shape, const2, pipeline_mode=pl.Buffered(1)),
            pl.BlockSpec(wt.shape, const2, pipeline_mode=pl.Buffered(1)),
            pl.BlockSpec((1, d), const2), pl.BlockSpec(gqb.shape, const2),
            pl.BlockSpec((1, 256), const2), pl.BlockSpec((1, 1024), const2),
            ws_spec, bs_spec,
        ],
        out_specs=out_specs,
        out_shape=out_shapes,
        compiler_params=pltpu.CompilerParams(
            dimension_semantics=("arbitrary",), vmem_limit_bytes=VMEM_LIMIT),
        name="proj_sample" if sample else "proj_prompt",
    )(x, wn, wt, gmix, gqb, gk, gsgu, ws, bs)


def _key_to_f32(key):
    bits = key ^ ((key >> 31) & INT_MAX)
    bits = jnp.where(jnp.logical_and(bits > 0, bits < KEY_MIN_NORMAL), KEY_MIN_NORMAL, bits)
    return lax.bitcast_convert_type(bits, F32)


def _select(score_ref, n_steps, topk, lo_hint, hi_hint):
    lanes = score_ref.shape[1]
    topk_f = float(topk)

    def count(pred):
        def body(t, acc):
            start = pl.multiple_of(t * SEL_ROWS, SEL_ROWS)
            tile = score_ref[pl.ds(start, SEL_ROWS), :]
            return acc + _fold8(jnp.where(pred(tile), 1.0, 0.0), jnp.add)
        acc = lax.fori_loop(0, n_steps, body, jnp.zeros((SUBLANES, lanes), F32))
        return _col_reduce8(acc, jnp.add)

    def update(st, mid):
        lo, hi, cnt_lo, cnt_hi = st
        thr_b = jnp.broadcast_to(_key_to_f32(mid), (SEL_ROWS, lanes))
        c = count(lambda tile: tile >= thr_b)
        ge = c >= topk_f
        lo2 = jnp.where(ge, mid, lo)
        hi2 = jnp.where(c == topk_f, mid + 1, jnp.where(ge, hi, mid))
        hi2 = jnp.where(jnp.logical_and(lo2 == 0, hi2 == KEY_MIN_NORMAL), 1, hi2)
        return lo2, hi2, jnp.where(ge, c, cnt_lo), jnp.where(ge, cnt_hi, c)

    def pivot(lo, hi):
        mid = (lo >> 1) + (hi >> 1) + (lo & hi & 1)
        mid = jnp.where(jnp.logical_and(lo < 0, hi > 0), 0, mid)
        return jnp.where(jnp.logical_and(lo < KEY_MIN_NORMAL, hi > KEY_MIN_NORMAL), KEY_MIN_NORMAL, mid)

    def cond(st):
        return jnp.logical_and(st[4] > 0.5, st[5] < 48 // SEL_STEPS_PER_CHECK)

    def body(st):
        lo, hi, cnt_lo, cnt_hi = lax.fori_loop(
            0, SEL_STEPS_PER_CHECK, lambda _, s: update(s, pivot(s[0], s[1])), st[:4])
        active = jnp.where(hi != lo + 1, 1.0, 0.0)
        return lo, hi, cnt_lo, cnt_hi, jnp.max(active), st[5] + 1

    st = (jnp.full((1, lanes), KEY_NEG_INF, I32), jnp.full((1, lanes), KEY_POS_INF + 1, I32),
          jnp.full((1, lanes), 2.0 * topk_f, F32), jnp.zeros((1, lanes), F32))
    for hint in (lo_hint, hi_hint):
        inside = jnp.logical_and(hint > st[0], hint < st[1])
        st = update(st, jnp.where(inside, hint, pivot(st[0], st[1])))
    lo, _, cnt_lo, cnt_hi, _, _ = lax.while_loop(cond, body, st + (jnp.float32(1.0), jnp.int32(0)))

    tie = jnp.logical_and(cnt_lo > topk_f, lo > KEY_NEG_INF)
    any_tie = jnp.max(jnp.where(tie, 1.0, 0.0))

    @pl.when(any_tie > 0.5)
    def _():
        tri = (lax.broadcasted_iota(I32, (SEL_ROWS, SEL_ROWS), 1)
               <= lax.broadcasted_iota(I32, (SEL_ROWS, SEL_ROWS), 0)).astype(BF16)
        at_b = jnp.broadcast_to(_key_to_f32(lo), (SEL_ROWS, lanes))
        next_b = jnp.broadcast_to(_key_to_f32(lo + 1), (SEL_ROWS, lanes))
        tie_b = jnp.broadcast_to(tie, (SEL_ROWS, lanes))
        need_b = jnp.broadcast_to(topk_f - cnt_hi, (SEL_ROWS, lanes))

        def resolve(t, seen):
            start = pl.multiple_of(t * SEL_ROWS, SEL_ROWS)
            tile = score_ref[pl.ds(start, SEL_ROWS), :]
            tied = jnp.logical_and(jnp.logical_and(tile >= at_b, tile < next_b), tie_b)
            upto = _dot(tri, jnp.where(tied, 1.0, 0.0).astype(BF16)) + seen
            score_ref[pl.ds(start, SEL_ROWS), :] = jnp.where(jnp.logical_and(tied, upto > need_b), -jnp.inf, tile)
            return upto[SEL_ROWS - 1:SEL_ROWS, :]

        lax.fori_loop(0, n_steps, resolve, jnp.zeros((1, lanes), F32))

    return jnp.maximum(_key_to_f32(lo), F32_LOWEST)


def _group_hints(gm):
    lo_hint = _to_key(_col_reduce8(_fold8(gm, jnp.minimum), jnp.minimum))
    hi_hint = _to_key(_col_reduce8(_fold8(gm, jnp.maximum), jnp.maximum)) + 1
    return lo_hint, hi_hint


def _attn_kernel(qt_ref, qi3t_ref, wit_ref, k_ref, vt_ref, ki3_ref, o_ref,
                 key_s, gm_s, wq_s, m_s, acc_s, sa_s, sb_s, la_s, lb_s, *, topk, ck):
    i = pl.program_id(1)
    n_ck = (i * QBLK + QBLK - 1) // ck + 1
    rs = 64
    gq = Q_PER_KV * QBLK

    wq_s[...] = jnp.zeros(wq_s.shape, BF16)
    for c in range(KV_HEADS):
        for g in range(Q_PER_KV):
            h = c * Q_PER_KV + g
            wq_s[c, c * HEAD_DIM:(c + 1) * HEAD_DIM, g * QBLK:(g + 1) * QBLK] = (
                qt_ref[h * HEAD_DIM:(h + 1) * HEAD_DIM, :])
    m_s[...] = jnp.full(m_s.shape, M_INIT, F32)
    acc_s[...] = jnp.zeros(acc_s.shape, F32)
    gm_s[...] = jnp.full(gm_s.shape, -jnp.inf, F32)

    qpos = lax.broadcasted_iota(I32, (rs, QBLK), 1) + i * QBLK
    krow = lax.broadcasted_iota(I32, (rs, QBLK), 0)
    hk = ck // 2
    seq = k_ref.shape[0]

    def idx_dot(row0, buf):
        buf[...] = _dot(ki3_ref[pl.ds(pl.multiple_of(row0, hk), hk), :], qi3t_ref[...])

    def to_keys(row0, buf):
        for p in range(hk // rs):
            acc = jnp.zeros((rs, QBLK), F32)
            for h in range(IDX_HEADS):
                acc = acc + jnp.maximum(buf[p * rs:(p + 1) * rs, h * QBLK:(h + 1) * QBLK], 0.0) * wit_ref[h:h + 1, :]
            sc = jnp.where(krow + (row0 + p * rs) <= qpos, acc, -jnp.inf)
            key_s[pl.ds(pl.multiple_of(row0 + p * rs, rs), rs), :] = sc
            gm_s[p * rs:(p + 1) * rs, :] = jnp.maximum(gm_s[p * rs:(p + 1) * rs, :], sc)

    idx_dot(0, la_s)

    def score_chunk(j, carry):
        base = j * ck
        idx_dot(base + hk, lb_s)
        to_keys(base, la_s)
        idx_dot(jnp.minimum(base + ck, seq - ck), la_s)
        to_keys(base + hk, lb_s)
        return carry

    lax.fori_loop(0, n_ck, score_chunk, 0)

    lo_hint, hi_hint = _group_hints(gm_s[...])
    tau = jnp.broadcast_to(_select(key_s, n_ck * (ck // SEL_ROWS), topk, lo_hint, hi_hint), (ck, QBLK))

    s_bufs = (sa_s, sb_s)
    last_ck = seq // ck - 1
    ones_rows = jnp.ones((ONES_ROWS, ck), BF16)

    def qk(j, c, buf):
        base = pl.multiple_of(j * ck, ck)
        buf[...] = _dot(k_ref[pl.ds(base, ck), :], wq_s[c])

    def softmax_pv(base, c, buf, bias):
        s = buf[...] + bias
        m_old = m_s[c]
        m_new = jnp.maximum(m_old, jnp.max(s, axis=0, keepdims=True))
        p = jnp.exp2(s - m_new).astype(BF16)
        alpha = jnp.exp2(m_old - m_new)
        m_s[c] = m_new
        vc = jnp.concatenate([vt_ref[c * HEAD_DIM:(c + 1) * HEAD_DIM, pl.ds(base, ck)], ones_rows], axis=0)
        acc_s[c] = alpha * acc_s[c] + _dot(vc, p)

    qk(0, 0, s_bufs[0])

    def attn_chunk(j, carry):
        base = pl.multiple_of(j * ck, ck)
        bias = jnp.where(key_s[pl.ds(base, ck), :] >= tau, 0.0, MASK_BIAS)
        bias = jnp.concatenate([bias] * Q_PER_KV, axis=1)
        for c in range(KV_HEADS):
            if c + 1 < KV_HEADS:
                qk(j, c + 1, s_bufs[(c + 1) % 2])
            else:
                qk(jnp.minimum(j + 1, last_ck), 0, s_bufs[0])
            softmax_pv(base, c, s_bufs[c % 2], bias)
        return carry

    lax.fori_loop(0, n_ck, attn_chunk, 0)

    for hp in range(N_HEADS // 2):
        parts = []
        for h in (2 * hp, 2 * hp + 1):
            c, g = divmod(h, Q_PER_KV)
            cols = slice(g * QBLK, (g + 1) * QBLK)
            parts.append(acc_s[c, 0:HEAD_DIM, cols] / acc_s[c, HEAD_DIM:HEAD_DIM + 1, cols])
        o_ref[:, hp * LANES:(hp + 1) * LANES] = jnp.concatenate(parts, axis=0).T


def _attn_prompt(qt, qi3t, wit, kb, vt, ki3, *, bsz, seq, topk, ck=512):
    nblk = seq // QBLK
    grid = (bsz, nblk)
    qblk = lambda bi, i: (bi * nblk + i, 0, 0)
    per_seq = lambda bi, i: (bi, 0, 0)
    gq = Q_PER_KV * QBLK
    return pl.pallas_call(
        functools.partial(_attn_kernel, topk=topk, ck=ck),
        grid=grid,
        in_specs=[
            pl.BlockSpec((None, 1024, QBLK), qblk),
            pl.BlockSpec((None, 256, IDX_HEADS * QBLK), qblk),
            pl.BlockSpec((None, IDX_HEADS, QBLK), qblk),
            pl.BlockSpec((None, seq, 256), per_seq, pipeline_mode=pl.Buffered(1)),
            pl.BlockSpec((256, seq), lambda bi, i: (0, bi), pipeline_mode=pl.Buffered(1)),
            pl.BlockSpec((None, seq, 256), per_seq, pipeline_mode=pl.Buffered(1)),
        ],
        out_specs=pl.BlockSpec((None, QBLK, 1024), lambda bi, i: (bi, i, 0)),
        out_shape=jax.ShapeDtypeStruct((bsz, seq, 1024), F32),
        scratch_shapes=[
            pltpu.VMEM((seq, QBLK), F32),
            pltpu.VMEM((SEL_GROUPS, QBLK), F32),
            pltpu.VMEM((KV_HEADS, 256, gq), BF16),
            pltpu.VMEM((KV_HEADS, 1, gq), F32),
            pltpu.VMEM((KV_HEADS, HEAD_DIM + ONES_ROWS, gq), F32),
            pltpu.VMEM((ck, gq), F32), pltpu.VMEM((ck, gq), F32),
            pltpu.VMEM((ck // 2, IDX_HEADS * QBLK), F32), pltpu.VMEM((ck // 2, IDX_HEADS * QBLK), F32),
        ],
        compiler_params=pltpu.CompilerParams(
            dimension_semantics=("arbitrary", "arbitrary"), vmem_limit_bytes=VMEM_LIMIT),
        name="attn_prompt",
    )(qt, qi3t, wit, kb, vt, ki3)


def _dec_score_kernel(pt_ref, qi3_ref, wcol_ref, kis_ref, *rest, n_pages, page):
    page_refs = rest[:n_pages]
    key_ref = rest[n_pages]
    del pt_ref
    n_keys = key_ref.shape[1]
    n_past = n_pages * page
    ki_all = jnp.concatenate(
        [r[...] for r in page_refs] + [jnp.broadcast_to(kis_ref[...], (IDX_DIM, n_keys - n_past))], axis=1)
    hk, lk = _hi_lo(ki_all)
    q3 = qi3_ref[...]
    qh = q3[:, 0:IDX_DIM]
    ql = q3[:, IDX_DIM:2 * IDX_DIM]
    lg = _dot(qh, hk) + _dot(ql, hk) + _dot(qh, lk)
    score = jnp.sum(jnp.maximum(lg, 0.0) * wcol_ref[...], axis=0, keepdims=True) + 0.0
    kpos = lax.broadcasted_iota(I32, (1, n_keys), 1)
    key_ref[...] = jnp.where(kpos <= n_past, score, -jnp.inf)


def _dec_select_kernel(key_ref, tau_ref, keyout_ref, *, topk):
    keyout_ref[...] = key_ref[...]
    gm = key_ref[0:SEL_GROUPS, :]
    for t in range(1, key_ref.shape[0] // SEL_GROUPS):
        gm = jnp.maximum(gm, key_ref[t * SEL_GROUPS:(t + 1) * SEL_GROUPS, :])
    lo_hint, hi_hint = _group_hints(gm)
    tau_ref[...] = _select(keyout_ref, key_ref.shape[0] // SEL_ROWS, topk, lo_hint, hi_hint)


def _dec_attn_kernel(pt_ref, lhs_ref, key_ref, tau_ref, ks_ref, vs_ref, *rest, n_pages, page):
    k_refs = rest[:n_pages]
    v_refs = rest[n_pages:2 * n_pages]
    o_ref = rest[2 * n_pages]
    del pt_ref
    n_past = n_pages * page
    lhs = lhs_ref[...]
    bias = jnp.where(key_ref[...] >= tau_ref[...], 0.0, MASK_BIAS)
    s = jnp.concatenate([_dot(lhs, r[...].astype(BF16)) for r in k_refs], axis=1)
    s = s + bias[:, 0:n_past]
    ks = ks_ref[...].astype(BF16).astype(F32)
    s_self = jnp.sum(lhs.astype(F32) * ks, axis=1, keepdims=True) + bias[:, n_past:n_past + 1]
    m = jnp.maximum(jnp.max(s, axis=1, keepdims=True), s_self)
    p = jnp.exp2(s - m)
    p_self = jnp.exp2(s_self - m)
    l = jnp.sum(p, axis=1, keepdims=True) + p_self
    vt_all = jnp.concatenate([r[...].astype(BF16) for r in v_refs], axis=1)
    vs = vs_ref[...].astype(BF16).astype(F32)
    o = (_dot_t(p.astype(BF16), vt_all) + p_self.astype(BF16).astype(F32) * vs) / l
    r_i = lax.broadcasted_iota(I32, o.shape, 0)
    l_i = lax.broadcasted_iota(I32, o.shape, 1)
    o = jnp.where((l_i >> HEAD_SHIFT) == (r_i >> 2), o, 0.0)
    o_ref[...] = (o[:, 0:HEAD_DIM] + o[:, HEAD_DIM:2 * HEAD_DIM]
                  + o[:, 2 * HEAD_DIM:3 * HEAD_DIM] + o[:, 3 * HEAD_DIM:4 * HEAD_DIM])


def _attn_decode(page_table, qt_s, qi3t_s, wit_s, kf_s, vf_s, kif_s, cache_k, cache_v, cache_kidx, *, topk):
    n, n_pages = page_table.shape
    n_phys, page = cache_k.shape[0], cache_k.shape[1]
    n_keys = -(-(n_pages * page + 1) // SEL_ROWS) * SEL_ROWS
    ckt = jnp.transpose(cache_k, (0, 2, 3, 1)).reshape(n_phys, KV_HEADS * HEAD_DIM, page)
    cvt = jnp.transpose(cache_v, (0, 2, 3, 1)).reshape(n_phys, KV_HEADS * HEAD_DIM, page)
    ckit = jnp.transpose(cache_kidx, (0, 2, 1))

    def page_spec(rows, p):
        return pl.BlockSpec((None, rows, page), lambda bi, pt, *_: (pt[bi, p], 0, 0))

    per_tok = lambda bi, *_: (bi, 0, 0)
    q_s = jnp.transpose(qt_s[0])
    qi3 = jnp.transpose(qi3t_s[0].reshape(256, IDX_HEADS, n), (2, 1, 0))
    wcol = jnp.transpose(wit_s[0])[:, :, None]
    qh = q_s.reshape(n, N_HEADS, 1, HEAD_DIM)
    c_of_head = (jnp.arange(N_HEADS) // Q_PER_KV)[None, :, None, None]
    lhs = jnp.where(c_of_head == jnp.arange(KV_HEADS)[None, None, :, None], qh,
                    jnp.zeros_like(qh)).reshape(n, N_HEADS, KV_HEADS * HEAD_DIM)

    keys = pl.pallas_call(
        functools.partial(_dec_score_kernel, n_pages=n_pages, page=page),
        grid_spec=pltpu.PrefetchScalarGridSpec(
            num_scalar_prefetch=1, grid=(n,),
            in_specs=[pl.BlockSpec((None, IDX_HEADS, 256), per_tok),
                      pl.BlockSpec((None, IDX_HEADS, 1), per_tok),
                      pl.BlockSpec((None, IDX_DIM, 1), per_tok)]
            + [page_spec(IDX_DIM, p) for p in range(n_pages)],
            out_specs=pl.BlockSpec((None, 1, n_keys), per_tok)),
        out_shape=jax.ShapeDtypeStruct((n, 1, n_keys), F32),
        compiler_params=pltpu.CompilerParams(dimension_semantics=("arbitrary",), vmem_limit_bytes=VMEM_LIMIT),
        name="dec_score",
    )(page_table, qi3, wcol, kif_s.reshape(n, IDX_DIM, 1), *([ckit] * n_pages))

    tau, keys_t = pl.pallas_call(
        functools.partial(_dec_select_kernel, topk=topk),
        out_shape=(jax.ShapeDtypeStruct((1, n), F32), jax.ShapeDtypeStruct((n_keys, n), F32)),
        name="dec_select",
    )(jnp.transpose(keys.reshape(n, n_keys)))
    keys = jnp.transpose(keys_t).reshape(n, 1, n_keys)

    o = pl.pallas_call(
        functools.partial(_dec_attn_kernel, n_pages=n_pages, page=page),
        grid_spec=pltpu.PrefetchScalarGridSpec(
            num_scalar_prefetch=1, grid=(n,),
            in_specs=[pl.BlockSpec((None, N_HEADS, 256), per_tok),
                      pl.BlockSpec((None, 1, n_keys), per_tok),
                      pl.BlockSpec((None, 1, 1), per_tok),
                      pl.BlockSpec((None, 1, 256), per_tok),
                      pl.BlockSpec((None, 1, 256), per_tok)]
            + [page_spec(KV_HEADS * HEAD_DIM, p) for p in range(n_pages)]
            + [page_spec(KV_HEADS * HEAD_DIM, p) for p in range(n_pages)],
            out_specs=pl.BlockSpec((None, N_HEADS, HEAD_DIM), per_tok)),
        out_shape=jax.ShapeDtypeStruct((n, N_HEADS, HEAD_DIM), F32),
        compiler_params=pltpu.CompilerParams(dimension_semantics=("arbitrary",), vmem_limit_bytes=VMEM_LIMIT),
        name="dec_attn",
    )(page_table, lhs, keys, tau.reshape(n, 1, 1), kf_s.reshape(n, 1, 256), vf_s.reshape(n, 1, 256),
      *([ckt] * n_pages), *([cvt] * n_pages))
    return o.reshape(n, N_HEADS * HEAD_DIM)


def _finish_kernel(x_ref, oatt_ref, sga_ref, sgu_ref, p_ref, wo_ref, gffn_ref, wup_ref, wdn_ref,
                   gple_ref, wpg_ref, wp_ref, y_ref):
    merged = (sga_ref[...] * oatt_ref[...] + sgu_ref[...]).astype(BF16)
    x = x_ref[...] + _dot(merged, wo_ref[...])
    hf = _rms_rows(x, gffn_ref[...]).astype(BF16)
    up = jnp.maximum(_dot(hf, wup_ref[...]), 0.0)
    x = x + _dot((up * up).astype(BF16), wdn_ref[...])
    hp = _rms_rows(x, gple_ref[...]).astype(BF16)
    gate = _sigmoid(_dot(hp, wpg_ref[...]))
    y_ref[...] = x + gate * _dot(p_ref[...].astype(BF16), wp_ref[...])


def _finish(x, oatt, sga, sgu, p, wo, gffn, wup, wdn, gple, wpg, wp, *, tm, name):
    t, d = x.shape
    row = lambda i: (i, 0)
    const = lambda i: (0, 0)

    def wspec(w):
        return pl.BlockSpec(w.shape, const, pipeline_mode=pl.Buffered(1))

    return pl.pallas_call(
        _finish_kernel,
        grid=(t // tm,),
        in_specs=[pl.BlockSpec((tm, d), row), pl.BlockSpec((tm, d), row), pl.BlockSpec((tm, d), row),
                  pl.BlockSpec((tm, d), row), pl.BlockSpec((tm, p.shape[1]), row),
                  wspec(wo), pl.BlockSpec((1, d), const), wspec(wup), wspec(wdn),
                  pl.BlockSpec((1, d), const), wspec(wpg), wspec(wp)],
        out_specs=pl.BlockSpec((tm, d), row),
        out_shape=jax.ShapeDtypeStruct((t, d), F32),
        compiler_params=pltpu.CompilerParams(
            dimension_semantics=("arbitrary",), vmem_limit_bytes=VMEM_LIMIT),
        name=name,
    )(x, oatt, sga, sgu, p, wo, gffn, wup, wdn, gple, wpg, wp)


def _pack_w_in(w_in):
    d = w_in.shape[0]
    splits = (1024, 256, 256, 512, 64, 8, 1024, 1024, 1024, 1024)
    offs = [0]
    for n in splits:
        offs.append(offs[-1] + n)
    wq, wk, wv, wqi, wki, wwi, wu, wvb, wga, wgb = [w_in[:, offs[i]:offs[i + 1]] for i in range(10)]
    wn = jnp.concatenate([wk, wv, wki, wki, wu, wvb, wga, wgb], axis=1).astype(BF16)
    wt = jnp.concatenate([wq, wv, wqi, wwi, jnp.zeros((d, _T_END - _T_WI - IDX_HEADS), w_in.dtype)],
                         axis=1).T.astype(BF16)
    return wn, wt


def kernel(x_prompt, x_sample, cache_k, cache_v, cache_kidx, page_table, p_prompt, p_sample, g_mix, w_in, g_q, g_k, g_sgu, w_s, b_s, w_o, g_ffn, w_up, w_down, g_ple, w_pg, w_p):
    depth = w_in.shape[0]
    bsz, seq, d = x_prompt.shape
    n_dec, dec_seq, _ = x_sample.shape
    assert dec_seq == 1 and seq % 512 == 0
    n_pages, page = page_table.shape[1], cache_k.shape[2]
    topk_p = min(TOPK_MAX, seq // 4)
    topk_s = min(TOPK_MAX, (n_pages * page + dec_seq) // 4)
    t_p = bsz * seq

    xp = x_prompt.reshape(t_p, d)
    xs = x_sample.reshape(n_dec, d)
    outs = [[] for _ in range(8)]
    for i in range(depth):
        wn, wt = _pack_w_in(w_in[i])
        gqb = jnp.broadcast_to((jnp.tile(g_q[i], N_HEADS) * (ATTN_SCALE * LOG2E))[:, None], (1024, QBLK))
        gk = jnp.tile(g_k[i], KV_HEADS)[None]
        gmix = g_mix[i][None]
        gsgu = g_sgu[i][None]
        fin_w = (w_o[i].astype(BF16), g_ffn[i][None], w_up[i].astype(BF16), w_down[i].astype(BF16),
                 g_ple[i][None], w_pg[i].astype(BF16), w_p[i].astype(BF16))

        (qt, kf, kb, vf, vt, qi3t, kif, ki3, wit, vn, sga, sgu) = _proj(
            xp, wn, wt, gmix, gqb, gk, gsgu, w_s[i], jnp.transpose(b_s[i]), sample=False, tm=256)
        oatt = _attn_prompt(qt, qi3t, wit, kb.reshape(bsz, seq, 256), vt, ki3.reshape(bsz, seq, 256),
                            bsz=bsz, seq=seq, topk=topk_p)
        xp = _finish(xp, oatt.reshape(t_p, 1024), sga, sgu, p_prompt[i].reshape(t_p, -1), *fin_w,
                     tm=256, name="finish_prompt")
        outs[0].append(kf.reshape(bsz, seq, KV_HEADS, HEAD_DIM))
        outs[1].append(vf.reshape(bsz, seq, KV_HEADS, HEAD_DIM))
        outs[2].append(kif.reshape(bsz, seq, IDX_DIM))
        outs[3].append(vn.reshape(bsz, seq, d))

        w00 = jnp.repeat(w_s[i][:, 0, 0], CHUNK)[None]
        b0 = jnp.repeat(b_s[i][:, 0], CHUNK)[None]
        (qt, kf, kb, vf, vt, qi3t, kif, ki3, wit, vn, sga, sgu) = _proj(
            xs, wn, wt, gmix, gqb, gk, gsgu, w00, b0, sample=True, tm=n_dec)
        oatt = _attn_decode(page_table, qt, qi3t, wit, kf, vf, kif, cache_k[i], cache_v[i], cache_kidx[i],
                            topk=topk_s)
        xs = _finish(xs, oatt, sga, sgu, p_sample[i].reshape(n_dec, -1), *fin_w, tm=n_dec,
                     name="finish_sample")
        outs[4].append(kf.reshape(n_dec, 1, KV_HEADS, HEAD_DIM))
        outs[5].append(vf.reshape(n_dec, 1, KV_HEADS, HEAD_DIM))
        outs[6].append(kif.reshape(n_dec, 1, IDX_DIM))
        outs[7].append(vn.reshape(n_dec, 1, d))

    return (xp.reshape(bsz, seq, d), xs.reshape(n_dec, 1, d), *[jnp.stack(o) for o in outs])
```

```python
import functools
import math

import jax
import jax.numpy as jnp
from jax import lax
from jax.experimental import pallas as pl
from jax.experimental.pallas import tpu as pltpu

F32 = jnp.float32
BF16 = jnp.bfloat16
I32 = jnp.int32

N_HEADS = 16
HEAD_DIM = 64
HEAD_SHIFT = 6
KV_HEADS = 4
Q_PER_KV = N_HEADS // KV_HEADS
IDX_HEADS = 8
IDX_DIM = 64
TOPK_MAX = 256
QBLK = 128
CHUNK = 128
SGU_GROUPS = 8
EPS = 1e-6
LOG2E = math.log2(math.e)
ATTN_SCALE = HEAD_DIM ** -0.5
IDX_SCALE = IDX_DIM ** -0.5
IDX_W_SCALE = IDX_HEADS ** -0.5

LANES = 128
SUBLANES = 8
INT_MAX = 2 ** 31 - 1
KEY_NEG_INF = -(2 ** 31) + 0x7FFFFF
KEY_POS_INF = 0x7F800000
KEY_MIN_NORMAL = 0x00800000
F32_LOWEST = -3.4028234663852886e38
MASK_BIAS = -1e30
M_INIT = -1e29
VMEM_LIMIT = 56 * 1024 * 1024
SEL_ROWS = 512
SEL_GROUPS = 256
SEL_STEPS_UNCHECKED = 9
SEL_STEPS_PER_CHECK = 2
HINT_DROP = 5
TIE_ROWS = 256
ONES_ROWS = 16
STEP_SPLIT = 1


def _dot(a, b):
    return jnp.dot(a, b, preferred_element_type=F32)


def _dot_t(a, b):
    return lax.dot_general(a, b, (((1,), (1,)), ((), ())), preferred_element_type=F32)


def _hi_lo(x):
    hi = x.astype(BF16)
    lo = (x - hi.astype(F32)).astype(BF16)
    return hi, lo


def _sigmoid(x):
    return 1.0 / (1.0 + jnp.exp(-x))


def _rms_rows(x, g):
    return x * lax.rsqrt(jnp.mean(x * x, axis=-1, keepdims=True) + EPS) * g


def _head_rms_scale(z):
    w = z.shape[1]
    seg = lax.broadcasted_iota(I32, (w, LANES), 0) >> HEAD_SHIFT
    col = lax.broadcasted_iota(I32, (w, LANES), 1)
    g = (seg == col).astype(BF16)
    row_t = lax.broadcasted_iota(I32, (LANES, w), 0)
    seg_t = lax.broadcasted_iota(I32, (LANES, w), 1) >> HEAD_SHIFT
    g_t = (row_t == seg_t).astype(BF16)
    hi, lo = _hi_lo(z * z)
    ssq = _dot(hi, g) + _dot(lo, g)
    r = lax.rsqrt(ssq * (1.0 / HEAD_DIM) + EPS)
    rh, rl = _hi_lo(r)
    return _dot(rh, g_t) + _dot(rl, g_t)


def _to_key(x):
    b = lax.bitcast_convert_type(x, I32)
    return b ^ ((b >> 31) & INT_MAX)


def _fold8(x, op):
    parts = [x[r * SUBLANES:(r + 1) * SUBLANES] for r in range(x.shape[0] // SUBLANES)]
    while len(parts) > 1:
        nxt = [op(parts[a], parts[a + 1]) for a in range(0, len(parts) - 1, 2)]
        if len(parts) % 2:
            nxt.append(parts[-1])
        parts = nxt
    return parts[0]


def _col_reduce8(x8, op):
    x8 = op(x8, pltpu.roll(x8, 4, 0))
    x8 = op(x8, pltpu.roll(x8, 2, 0))
    x8 = op(x8, pltpu.roll(x8, 1, 0))
    return x8[0:1]


_N_K, _N_V, _N_KI, _N_U, _N_VB, _N_GA, _N_GB, _N_END = 0, 256, 512, 640, 1664, 2688, 3712, 4736
_T_Q, _T_V, _T_QI, _T_WI, _T_END = 0, 1024, 1280, 1792, 1808


def _proj_kernel(x_ref, wn_ref, wt_ref, gmix_ref, gqb_ref, gk_ref, gsgu_ref, ws_ref, bs_ref,
                 qt_ref, kf_ref, kb_ref, vf_ref, vt_ref, qi3t_ref, kif_ref, ki3_ref, wit_ref,
                 vn_ref, sga_ref, sgu_ref, *, sample):
    tm = x_ref.shape[0]
    nq = tm // QBLK
    xn = _rms_rows(x_ref[...], gmix_ref[...]).astype(BF16)

    def nat(lo, hi):
        return _dot(xn, wn_ref[:, lo:hi])

    def tra(lo, hi):
        return _dot_t(wt_ref[lo:hi, :], xn)

    z3 = tra(_T_Q, _T_V).reshape(N_HEADS, HEAD_DIM, tm)
    r = lax.rsqrt(jnp.mean(z3 * z3, axis=1, keepdims=True) + EPS)
    qn = (z3 * r).reshape(N_HEADS * HEAD_DIM, tm)
    for n in range(nq):
        qt_ref[n] = (qn[:, n * QBLK:(n + 1) * QBLK] * gqb_ref[...]).astype(BF16)
    vt_ref[...] = tra(_T_V, _T_QI).astype(BF16)

    zqi = tra(_T_QI, _T_WI) * IDX_SCALE
    zero = jnp.zeros((IDX_DIM, tm), BF16)
    for h in range(IDX_HEADS):
        hq, lq = _hi_lo(zqi[h * IDX_DIM:(h + 1) * IDX_DIM])
        blk = jnp.concatenate([hq, lq, hq, zero], axis=0)
        for n in range(nq):
            qi3t_ref[n, :, h * QBLK:(h + 1) * QBLK] = blk[:, n * QBLK:(n + 1) * QBLK]
    zwi = tra(_T_WI, _T_END) * IDX_W_SCALE
    for n in range(nq):
        wit_ref[n] = zwi[0:IDX_HEADS, n * QBLK:(n + 1) * QBLK]

    zk = nat(_N_K, _N_V)
    kn = zk * _head_rms_scale(zk) * gk_ref[...]
    kf_ref[...] = kn
    kb_ref[...] = kn.astype(BF16)
    vf_ref[...] = nat(_N_V, _N_KI)
    first = lax.broadcasted_iota(I32, (tm, LANES), 1) < IDX_DIM
    zki = nat(_N_KI, _N_U)
    kif_ref[...] = zki[:, 0:IDX_DIM]
    hk, lk = _hi_lo(zki)
    ki3_ref[:, 0:LANES] = hk
    ki3_ref[:, LANES:2 * LANES] = jnp.where(first, lk, jnp.zeros_like(lk))

    u = jax.nn.gelu(nat(_N_U, _N_VB))
    vn = _rms_rows(jax.nn.gelu(nat(_N_VB, _N_GA)), gsgu_ref[...])
    vn_ref[...] = vn
    sga_ref[...] = _sigmoid(nat(_N_GA, _N_GB))
    sgb = _sigmoid(nat(_N_GB, _N_END))
    if sample:
        sgu_ref[...] = sgb * (u * (vn * ws_ref[...] + bs_ref[...]))
    else:
        tril = (lax.broadcasted_iota(I32, (CHUNK, CHUNK), 0)
                >= lax.broadcasted_iota(I32, (CHUNK, CHUNK), 1))
        for g in range(SGU_GROUPS):
            wg = jnp.where(tril, ws_ref[g], 0.0).astype(BF16)
            bg = bs_ref[:, g:g + 1]
            for n in range(tm // CHUNK):
                rows = slice(n * CHUNK, (n + 1) * CHUNK)
                cols = slice(g * CHUNK, (g + 1) * CHUNK)
                mixed = _dot(wg, vn[rows, cols].astype(BF16)) + bg
                sgu_ref[rows, cols] = sgb[rows, cols] * (u[rows, cols] * mixed)


def _proj(x, wn, wt, gmix, gqb, gk, gsgu, ws, bs, *, sample, tm):
    t = x.shape[0]
    d = x.shape[1]
    nq = tm // QBLK
    row = lambda i: (i, 0)
    const2 = lambda i: (0, 0)
    blk3 = lambda i: (i, 0, 0)
    if sample:
        ws_spec = pl.BlockSpec(ws.shape, const2)
    else:
        ws_spec = pl.BlockSpec(ws.shape, lambda i: (0, 0, 0))
    bs_spec = pl.BlockSpec(bs.shape, const2)
    out_shapes = (
        jax.ShapeDtypeStruct((t // QBLK, 1024, QBLK), BF16),
        jax.ShapeDtypeStruct((t, 256), F32), jax.ShapeDtypeStruct((t, 256), BF16),
        jax.ShapeDtypeStruct((t, 256), F32), jax.ShapeDtypeStruct((256, t), BF16),
        jax.ShapeDtypeStruct((t // QBLK, 256, IDX_HEADS * QBLK), BF16),
        jax.ShapeDtypeStruct((t, IDX_DIM), F32),
        jax.ShapeDtypeStruct((t, 256), BF16),
        jax.ShapeDtypeStruct((t // QBLK, IDX_HEADS, QBLK), F32),
        jax.ShapeDtypeStruct((t, 1024), F32),
        jax.ShapeDtypeStruct((t, 1024), F32),
        jax.ShapeDtypeStruct((t, 1024), F32),
    )
    out_specs = (
        pl.BlockSpec((nq, 1024, QBLK), blk3),
        pl.BlockSpec((tm, 256), row), pl.BlockSpec((tm, 256), row),
        pl.BlockSpec((tm, 256), row), pl.BlockSpec((256, tm), lambda i: (0, i)),
        pl.BlockSpec((nq, 256, IDX_HEADS * QBLK), blk3),
        pl.BlockSpec((tm, IDX_DIM), row),
        pl.BlockSpec((tm, 256), row),
        pl.BlockSpec((nq, IDX_HEADS, QBLK), blk3),
        pl.BlockSpec((tm, 1024), row),
        pl.BlockSpec((tm, 1024), row),
        pl.BlockSpec((tm, 1024), row),
    )
    return pl.pallas_call(
        functools.partial(_proj_kernel, sample=sample),
        grid=(t // tm,),
        in_specs=[
            pl.BlockSpec((tm, d), row),
            pl.BlockSpec(wn.shape, const2, pipeline_mode=pl.Buffered(1)),
            pl.BlockSpec(wt.shape, const2, pipeline_mode=pl.Buffered(1)),
            pl.BlockSpec((1, d), const2), pl.BlockSpec(gqb.shape, const2),
            pl.BlockSpec((1, 256), const2), pl.BlockSpec((1, 1024), const2),
            ws_spec, bs_spec,
        ],
        out_specs=out_specs,
        out_shape=out_shapes,
        compiler_params=pltpu.CompilerParams(
            dimension_semantics=("arbitrary",), vmem_limit_bytes=VMEM_LIMIT),
        name="proj_sample" if sample else "proj_prompt",
    )(x, wn, wt, gmix, gqb, gk, gsgu, ws, bs)


def _key_to_f32(key):
    bits = key ^ ((key >> 31) & INT_MAX)
    bits = jnp.where(jnp.logical_and(bits > 0, bits < KEY_MIN_NORMAL), KEY_MIN_NORMAL, bits)
    return lax.bitcast_convert_type(bits, F32)


def _select(score_ref, n_steps, topk, lo_hint, hi_hint):
    lanes = score_ref.shape[1]
    topk_f = float(topk)

    def count(pred):
        def body(t, acc):
            start = pl.multiple_of(t * SEL_ROWS, SEL_ROWS)
            tile = score_ref[pl.ds(start, SEL_ROWS), :]
            return acc + _fold8(jnp.where(pred(tile), 1.0, 0.0), jnp.add)
        acc = lax.fori_loop(0, n_steps, body, jnp.zeros((SUBLANES, lanes), F32))
        return _col_reduce8(acc, jnp.add)

    def update(st, mid):
        lo, hi, cnt_lo, cnt_hi = st
        thr_b = jnp.broadcast_to(_key_to_f32(mid), (SEL_ROWS, lanes))
        c = count(lambda tile: tile >= thr_b)
        ge = c >= topk_f
        lo2 = jnp.where(ge, mid, lo)
        hi2 = jnp.where(c == topk_f, mid + 1, jnp.where(ge, hi, mid))
        hi2 = jnp.where(jnp.logical_and(lo2 == 0, hi2 == KEY_MIN_NORMAL), 1, hi2)
        return lo2, hi2, jnp.where(ge, c, cnt_lo), jnp.where(ge, cnt_hi, c)

    def pivot(lo, hi):
        mid = (lo >> 1) + (hi >> 1) + (lo & hi & 1)
        mid = jnp.where(jnp.logical_and(lo < 0, hi > 0), 0, mid)
        return jnp.where(jnp.logical_and(lo < KEY_MIN_NORMAL, hi > KEY_MIN_NORMAL), KEY_MIN_NORMAL, mid)

    def cond(st):
        return jnp.logical_and(st[4] > 0.5, st[5] < 48 // SEL_STEPS_PER_CHECK)

    def body(st):
        lo, hi, cnt_lo, cnt_hi = lax.fori_loop(
            0, SEL_STEPS_PER_CHECK, lambda _, s: update(s, pivot(s[0], s[1])), st[:4])
        active = jnp.where(hi != lo + 1, 1.0, 0.0)
        return lo, hi, cnt_lo, cnt_hi, jnp.max(active), st[5] + 1

    st = (jnp.full((1, lanes), KEY_NEG_INF, I32), jnp.full((1, lanes), KEY_POS_INF + 1, I32),
          jnp.full((1, lanes), 2.0 * topk_f, F32), jnp.zeros((1, lanes), F32))
    scaled = jnp.where(hi_hint > (HINT_DROP << 23) + KEY_MIN_NORMAL, hi_hint - (HINT_DROP << 23), KEY_NEG_INF)
    for hint in (lo_hint, hi_hint, scaled):
        inside = jnp.logical_and(hint > st[0], hint < st[1])
        st = update(st, jnp.where(inside, hint, pivot(st[0], st[1])))
    st = lax.fori_loop(0, SEL_STEPS_UNCHECKED, lambda _, s: update(s, pivot(s[0], s[1])), st)
    lo, _, cnt_lo, cnt_hi, _, _ = lax.while_loop(cond, body, st + (jnp.float32(1.0), jnp.int32(0)))

    tie = jnp.logical_and(cnt_lo > topk_f, lo > KEY_NEG_INF)
    any_tie = jnp.max(jnp.where(tie, 1.0, 0.0))

    @pl.when(any_tie > 0.5)
    def _():
        tri = (lax.broadcasted_iota(I32, (TIE_ROWS, TIE_ROWS), 1)
               <= lax.broadcasted_iota(I32, (TIE_ROWS, TIE_ROWS), 0)).astype(BF16)
        at_b = jnp.broadcast_to(_key_to_f32(lo), (TIE_ROWS, lanes))
        next_b = jnp.broadcast_to(_key_to_f32(lo + 1), (TIE_ROWS, lanes))
        tie_b = jnp.broadcast_to(tie, (TIE_ROWS, lanes))
        need_b = jnp.broadcast_to(topk_f - cnt_hi, (TIE_ROWS, lanes))

        def resolve(t, seen):
            for part in range(SEL_ROWS // TIE_ROWS):
                start = pl.multiple_of(t * SEL_ROWS + part * TIE_ROWS, TIE_ROWS)
                tile = score_ref[pl.ds(start, TIE_ROWS), :]
                tied = jnp.logical_and(jnp.logical_and(tile >= at_b, tile < next_b), tie_b)
                ones = jnp.where(tied, 1.0, 0.0)
                upto = _dot(tri, ones.astype(BF16)) + seen
                score_ref[pl.ds(start, TIE_ROWS), :] = jnp.where(jnp.logical_and(tied, upto > need_b), -jnp.inf, tile)
                seen = seen + _col_reduce8(_fold8(ones, jnp.add), jnp.add)
            return seen

        lax.fori_loop(0, n_steps, resolve, jnp.zeros((1, lanes), F32))

    return jnp.maximum(_key_to_f32(lo), F32_LOWEST)


def _group_hints(gm):
    lo_hint = _to_key(_col_reduce8(_fold8(gm, jnp.minimum), jnp.minimum))
    hi_hint = _to_key(_col_reduce8(_fold8(gm, jnp.maximum), jnp.maximum)) + 1
    return lo_hint, hi_hint


def _attn_kernel(qt_ref, qi3t_ref, wit_ref, k_ref, vt_ref, ki3_ref, o_ref,
                 key_s, gm_s, wq_s, m_s, acc_s, sa_s, sb_s, la_s, lb_s, *, topk, ck):
    i = pl.program_id(1)
    n_ck = (i * QBLK + QBLK - 1) // ck + 1
    rs = 64
    gq = Q_PER_KV * QBLK

    wq_s[...] = jnp.zeros(wq_s.shape, BF16)
    for c in range(KV_HEADS):
        for g in range(Q_PER_KV):
            h = c * Q_PER_KV + g
            wq_s[c, c * HEAD_DIM:(c + 1) * HEAD_DIM, g * QBLK:(g + 1) * QBLK] = (
                qt_ref[h * HEAD_DIM:(h + 1) * HEAD_DIM, :])
    m_s[...] = jnp.full(m_s.shape, M_INIT, F32)
    acc_s[...] = jnp.zeros(acc_s.shape, F32)
    gm_s[...] = jnp.full(gm_s.shape, -jnp.inf, F32)

    qpos = lax.broadcasted_iota(I32, (rs, QBLK), 1) + i * QBLK
    krow = lax.broadcasted_iota(I32, (rs, QBLK), 0)
    hk = ck // 2
    seq = k_ref.shape[0]

    def idx_dot(row0, buf):
        buf[...] = _dot(ki3_ref[pl.ds(pl.multiple_of(row0, hk), hk), :], qi3t_ref[...])

    def to_keys(row0, buf):
        for p in range(hk // rs):
            acc = jnp.zeros((rs, QBLK), F32)
            for h in range(IDX_HEADS):
                acc = acc + jnp.maximum(buf[p * rs:(p + 1) * rs, h * QBLK:(h + 1) * QBLK], 0.0) * wit_ref[h:h + 1, :]
            sc = jnp.where(krow + (row0 + p * rs) <= qpos, acc, -jnp.inf)
            key_s[pl.ds(pl.multiple_of(row0 + p * rs, rs), rs), :] = sc
            gm_s[p * rs:(p + 1) * rs, :] = jnp.maximum(gm_s[p * rs:(p + 1) * rs, :], sc)

    idx_dot(0, la_s)

    def score_chunk(j, j_next):
        base = j * ck
        idx_dot(base + hk, lb_s)
        to_keys(base, la_s)
        idx_dot(j_next * ck, la_s)
        to_keys(base + hk, lb_s)

    def score_pair(t, carry):
        score_chunk(2 * t, 2 * t + 1)
        score_chunk(2 * t + 1, jnp.minimum(2 * t + 2, seq // ck - 1))
        return carry

    lax.fori_loop(0, n_ck // 2, score_pair, 0)

    @pl.when(n_ck % 2 == 1)
    def _():
        score_chunk(n_ck - 1, n_ck - 1)

    lo_hint, hi_hint = _group_hints(gm_s[...])
    tau = jnp.broadcast_to(_select(key_s, n_ck * (ck // SEL_ROWS), topk, lo_hint, hi_hint), (ck, QBLK))

    s_bufs = (sa_s, sb_s)
    last_ck = seq // ck - 1
    ones_rows = jnp.ones((ONES_ROWS, ck), BF16)

    n_step = KV_HEADS * STEP_SPLIT
    sw = gq // STEP_SPLIT

    def qk(j, t, buf):
        c, part = divmod(t, STEP_SPLIT)
        base = pl.multiple_of(j * ck, ck)
        buf[...] = _dot(k_ref[pl.ds(base, ck), :], wq_s[c, :, part * sw:(part + 1) * sw])

    def softmax_pv(base, t, buf, bias):
        c, part = divmod(t, STEP_SPLIT)
        cols = slice(part * sw, (part + 1) * sw)
        s = buf[...] + bias
        m_old = m_s[c, :, cols]
        m_new = jnp.maximum(m_old, jnp.max(s, axis=0, keepdims=True))
        p = jnp.exp2(s - m_new).astype(BF16)
        alpha = jnp.exp2(m_old - m_new)
        m_s[c, :, cols] = m_new
        vc = jnp.concatenate([vt_ref[c * HEAD_DIM:(c + 1) * HEAD_DIM, pl.ds(base, ck)], ones_rows], axis=0)
        acc_s[c, :, cols] = alpha * acc_s[c, :, cols] + _dot(vc, p)

    qk(0, 0, s_bufs[0])

    def attn_chunk(j, j_next):
        base = pl.multiple_of(j * ck, ck)
        bias = jnp.where(key_s[pl.ds(base, ck), :] >= tau, 0.0, MASK_BIAS)
        bias = jnp.concatenate([bias] * (sw // QBLK), axis=1)
        for t in range(n_step):
            if t + 1 < n_step:
                qk(j, t + 1, s_bufs[(t + 1) % 2])
            else:
                qk(j_next, 0, s_bufs[0])
            softmax_pv(base, t, s_bufs[t % 2], bias)

    def attn_pair(t, carry):
        attn_chunk(2 * t, 2 * t + 1)
        attn_chunk(2 * t + 1, jnp.minimum(2 * t + 2, last_ck))
        return carry

    lax.fori_loop(0, n_ck // 2, attn_pair, 0)

    @pl.when(n_ck % 2 == 1)
    def _():
        attn_chunk(n_ck - 1, n_ck - 1)

    for hp in range(N_HEADS // 2):
        parts = []
        for h in (2 * hp, 2 * hp + 1):
            c, g = divmod(h, Q_PER_KV)
            cols = slice(g * QBLK, (g + 1) * QBLK)
            parts.append(acc_s[c, 0:HEAD_DIM, cols] / acc_s[c, HEAD_DIM:HEAD_DIM + 1, cols])
        o_ref[:, hp * LANES:(hp + 1) * LANES] = jnp.concatenate(parts, axis=0).T


def _attn_prompt(qt, qi3t, wit, kb, vt, ki3, *, bsz, seq, topk, ck=512):
    nblk = seq // QBLK
    grid = (bsz, nblk)
    qblk = lambda bi, i: (bi * nblk + i, 0, 0)
    per_seq = lambda bi, i: (bi, 0, 0)
    gq = Q_PER_KV * QBLK
    return pl.pallas_call(
        functools.partial(_attn_kernel, topk=topk, ck=ck),
        grid=grid,
        in_specs=[
            pl.BlockSpec((None, 1024, QBLK), qblk),
            pl.BlockSpec((None, 256, IDX_HEADS * QBLK), qblk),
            pl.BlockSpec((None, IDX_HEADS, QBLK), qblk),
            pl.BlockSpec((None, seq, 256), per_seq, pipeline_mode=pl.Buffered(1)),
            pl.BlockSpec((256, seq), lambda bi, i: (0, bi), pipeline_mode=pl.Buffered(1)),
            pl.BlockSpec((None, seq, 256), per_seq, pipeline_mode=pl.Buffered(1)),
        ],
        out_specs=pl.BlockSpec((None, QBLK, 1024), lambda bi, i: (bi, i, 0)),
        out_shape=jax.ShapeDtypeStruct((bsz, seq, 1024), F32),
        scratch_shapes=[
            pltpu.VMEM((seq, QBLK), F32),
            pltpu.VMEM((SEL_GROUPS, QBLK), F32),
            pltpu.VMEM((KV_HEADS, 256, gq), BF16),
            pltpu.VMEM((KV_HEADS, 1, gq), F32),
            pltpu.VMEM((KV_HEADS, HEAD_DIM + ONES_ROWS, gq), F32),
            pltpu.VMEM((ck, gq // STEP_SPLIT), F32), pltpu.VMEM((ck, gq // STEP_SPLIT), F32),
            pltpu.VMEM((ck // 2, IDX_HEADS * QBLK), F32), pltpu.VMEM((ck // 2, IDX_HEADS * QBLK), F32),
        ],
        compiler_params=pltpu.CompilerParams(
            dimension_semantics=("arbitrary", "arbitrary"), vmem_limit_bytes=VMEM_LIMIT),
        name="attn_prompt",
    )(qt, qi3t, wit, kb, vt, ki3)


def _dec_score_kernel(pt_ref, qi3_ref, wcol_ref, kis_ref, *rest, n_pages, page):
    page_refs = rest[:n_pages]
    key_ref = rest[n_pages]
    del pt_ref
    n_keys = key_ref.shape[1]
    n_past = n_pages * page
    ki_all = jnp.concatenate(
        [r[...] for r in page_refs] + [jnp.broadcast_to(kis_ref[...], (IDX_DIM, n_keys - n_past))], axis=1)
    hk, lk = _hi_lo(ki_all)
    q3 = qi3_ref[...]
    qh = q3[:, 0:IDX_DIM]
    ql = q3[:, IDX_DIM:2 * IDX_DIM]
    lg = _dot(qh, hk) + _dot(ql, hk) + _dot(qh, lk)
    score = jnp.sum(jnp.maximum(lg, 0.0) * wcol_ref[...], axis=0, keepdims=True) + 0.0
    kpos = lax.broadcasted_iota(I32, (1, n_keys), 1)
    key_ref[...] = jnp.where(kpos <= n_past, score, -jnp.inf)


def _dec_select_kernel(key_ref, tau_ref, keyout_ref, *, topk):
    keyout_ref[...] = key_ref[...]
    gm = key_ref[0:SEL_GROUPS, :]
    for t in range(1, key_ref.shape[0] // SEL_GROUPS):
        gm = jnp.maximum(gm, key_ref[t * SEL_GROUPS:(t + 1) * SEL_GROUPS, :])
    lo_hint, hi_hint = _group_hints(gm)
    tau_ref[...] = _select(keyout_ref, key_ref.shape[0] // SEL_ROWS, topk, lo_hint, hi_hint)


def _dec_attn_kernel(pt_ref, lhs_ref, key_ref, tau_ref, ks_ref, vs_ref, *rest, n_pages, page):
    k_refs = rest[:n_pages]
    v_refs = rest[n_pages:2 * n_pages]
    o_ref = rest[2 * n_pages]
    del pt_ref
    n_past = n_pages * page
    lhs = lhs_ref[...]
    bias = jnp.where(key_ref[...] >= tau_ref[...], 0.0, MASK_BIAS)
    s = jnp.concatenate([_dot(lhs, r[...].astype(BF16)) for r in k_refs], axis=1)
    s = s + bias[:, 0:n_past]
    ks = ks_ref[...].astype(BF16).astype(F32)
    s_self = jnp.sum(lhs.astype(F32) * ks, axis=1, keepdims=True) + bias[:, n_past:n_past + 1]
    m = jnp.maximum(jnp.max(s, axis=1, keepdims=True), s_self)
    p = jnp.exp2(s - m)
    p_self = jnp.exp2(s_self - m)
    l = jnp.sum(p, axis=1, keepdims=True) + p_self
    vt_all = jnp.concatenate([r[...].astype(BF16) for r in v_refs], axis=1)
    vs = vs_ref[...].astype(BF16).astype(F32)
    o = (_dot_t(p.astype(BF16), vt_all) + p_self.astype(BF16).astype(F32) * vs) / l
    r_i = lax.broadcasted_iota(I32, o.shape, 0)
    l_i = lax.broadcasted_iota(I32, o.shape, 1)
    o = jnp.where((l_i >> HEAD_SHIFT) == (r_i >> 2), o, 0.0)
    o_ref[...] = (o[:, 0:HEAD_DIM] + o[:, HEAD_DIM:2 * HEAD_DIM]
                  + o[:, 2 * HEAD_DIM:3 * HEAD_DIM] + o[:, 3 * HEAD_DIM:4 * HEAD_DIM])


def _attn_decode(page_table, qt_s, qi3t_s, wit_s, kf_s, vf_s, kif_s, cache_k, cache_v, cache_kidx, *, topk):
    n, n_pages = page_table.shape
    n_phys, page = cache_k.shape[0], cache_k.shape[1]
    n_keys = -(-(n_pages * page + 1) // SEL_ROWS) * SEL_ROWS
    ckt = jnp.transpose(cache_k, (0, 2, 3, 1)).reshape(n_phys, KV_HEADS * HEAD_DIM, page)
    cvt = jnp.transpose(cache_v, (0, 2, 3, 1)).reshape(n_phys, KV_HEADS * HEAD_DIM, page)
    ckit = jnp.transpose(cache_kidx, (0, 2, 1))

    def page_spec(rows, p):
        return pl.BlockSpec((None, rows, page), lambda bi, pt, *_: (pt[bi, p], 0, 0))

    per_tok = lambda bi, *_: (bi, 0, 0)
    q_s = jnp.transpose(qt_s[0])
    qi3 = jnp.transpose(qi3t_s[0].reshape(256, IDX_HEADS, n), (2, 1, 0))
    wcol = jnp.transpose(wit_s[0])[:, :, None]
    qh = q_s.reshape(n, N_HEADS, 1, HEAD_DIM)
    c_of_head = (jnp.arange(N_HEADS) // Q_PER_KV)[None, :, None, None]
    lhs = jnp.where(c_of_head == jnp.arange(KV_HEADS)[None, None, :, None], qh,
                    jnp.zeros_like(qh)).reshape(n, N_HEADS, KV_HEADS * HEAD_DIM)

    keys = pl.pallas_call(
        functools.partial(_dec_score_kernel, n_pages=n_pages, page=page),
        grid_spec=pltpu.PrefetchScalarGridSpec(
            num_scalar_prefetch=1, grid=(n,),
            in_specs=[pl.BlockSpec((None, IDX_HEADS, 256), per_tok),
                      pl.BlockSpec((None, IDX_HEADS, 1), per_tok),
                      pl.BlockSpec((None, IDX_DIM, 1), per_tok)]
            + [page_spec(IDX_DIM, p) for p in range(n_pages)],
            out_specs=pl.BlockSpec((None, 1, n_keys), per_tok)),
        out_shape=jax.ShapeDtypeStruct((n, 1, n_keys), F32),
        compiler_params=pltpu.CompilerParams(dimension_semantics=("arbitrary",), vmem_limit_bytes=VMEM_LIMIT),
        name="dec_score",
    )(page_table, qi3, wcol, kif_s.reshape(n, IDX_DIM, 1), *([ckit] * n_pages))

    tau, keys_t = pl.pallas_call(
        functools.partial(_dec_select_kernel, topk=topk),
        out_shape=(jax.ShapeDtypeStruct((1, n), F32), jax.ShapeDtypeStruct((n_keys, n), F32)),
        name="dec_select",
    )(jnp.transpose(keys.reshape(n, n_keys)))
    keys = jnp.transpose(keys_t).reshape(n, 1, n_keys)

    o = pl.pallas_call(
        functools.partial(_dec_attn_kernel, n_pages=n_pages, page=page),
        grid_spec=pltpu.PrefetchScalarGridSpec(
            num_scalar_prefetch=1, grid=(n,),
            in_specs=[pl.BlockSpec((None, N_HEADS, 256), per_tok),
                      pl.BlockSpec((None, 1, n_keys), per_tok),
                      pl.BlockSpec((None, 1, 1), per_tok),
                      pl.BlockSpec((None, 1, 256), per_tok),
                      pl.BlockSpec((None, 1, 256), per_tok)]
            + [page_spec(KV_HEADS * HEAD_DIM, p) for p in range(n_pages)]
            + [page_spec(KV_HEADS * HEAD_DIM, p) for p in range(n_pages)],
            out_specs=pl.BlockSpec((None, N_HEADS, HEAD_DIM), per_tok)),
        out_shape=jax.ShapeDtypeStruct((n, N_HEADS, HEAD_DIM), F32),
        compiler_params=pltpu.CompilerParams(dimension_semantics=("arbitrary",), vmem_limit_bytes=VMEM_LIMIT),
        name="dec_attn",
    )(page_table, lhs, keys, tau.reshape(n, 1, 1), kf_s.reshape(n, 1, 256), vf_s.reshape(n, 1, 256),
      *([ckt] * n_pages), *([cvt] * n_pages))
    return o.reshape(n, N_HEADS * HEAD_DIM)


def _finish_kernel(x_ref, oatt_ref, sga_ref, sgu_ref, p_ref, wo_ref, gffn_ref, wup_ref, wdn_ref,
                   gple_ref, wpg_ref, wp_ref, y_ref):
    merged = (sga_ref[...] * oatt_ref[...] + sgu_ref[...]).astype(BF16)
    x = x_ref[...] + _dot(merged, wo_ref[...])
    hf = _rms_rows(x, gffn_ref[...]).astype(BF16)
    up = jnp.maximum(_dot(hf, wup_ref[...]), 0.0)
    x = x + _dot((up * up).astype(BF16), wdn_ref[...])
    hp = _rms_rows(x, gple_ref[...]).astype(BF16)
    gate = _sigmoid(_dot(hp, wpg_ref[...]))
    y_ref[...] = x + gate * _dot(p_ref[...].astype(BF16), wp_ref[...])


def _finish(x, oatt, sga, sgu, p, wo, gffn, wup, wdn, gple, wpg, wp, *, tm, name):
    t, d = x.shape
    row = lambda i: (i, 0)
    const = lambda i: (0, 0)

    def wspec(w):
        return pl.BlockSpec(w.shape, const, pipeline_mode=pl.Buffered(1))

    return pl.pallas_call(
        _finish_kernel,
        grid=(t // tm,),
        in_specs=[pl.BlockSpec((tm, d), row), pl.BlockSpec((tm, d), row), pl.BlockSpec((tm, d), row),
                  pl.BlockSpec((tm, d), row), pl.BlockSpec((tm, p.shape[1]), row),
                  wspec(wo), pl.BlockSpec((1, d), const), wspec(wup), wspec(wdn),
                  pl.BlockSpec((1, d), const), wspec(wpg), wspec(wp)],
        out_specs=pl.BlockSpec((tm, d), row),
        out_shape=jax.ShapeDtypeStruct((t, d), F32),
        compiler_params=pltpu.CompilerParams(
            dimension_semantics=("arbitrary",), vmem_limit_bytes=VMEM_LIMIT),
        name=name,
    )(x, oatt, sga, sgu, p, wo, gffn, wup, wdn, gple, wpg, wp)


def _pack_w_in(w_in):
    d = w_in.shape[0]
    splits = (1024, 256, 256, 512, 64, 8, 1024, 1024, 1024, 1024)
    offs = [0]
    for n in splits:
        offs.append(offs[-1] + n)
    wq, wk, wv, wqi, wki, wwi, wu, wvb, wga, wgb = [w_in[:, offs[i]:offs[i + 1]] for i in range(10)]
    wn = jnp.concatenate([wk, wv, wki, wki, wu, wvb, wga, wgb], axis=1).astype(BF16)
    wt = jnp.concatenate([wq, wv, wqi, wwi, jnp.zeros((d, _T_END - _T_WI - IDX_HEADS), w_in.dtype)],
                         axis=1).T.astype(BF16)
    return wn, wt


def kernel(x_prompt, x_sample, cache_k, cache_v, cache_kidx, page_table, p_prompt, p_sample, g_mix, w_in, g_q, g_k, g_sgu, w_s, b_s, w_o, g_ffn, w_up, w_down, g_ple, w_pg, w_p):
    depth = w_in.shape[0]
    bsz, seq, d = x_prompt.shape
    n_dec, dec_seq, _ = x_sample.shape
    assert dec_seq == 1 and seq % 512 == 0
    n_pages, page = page_table.shape[1], cache_k.shape[2]
    topk_p = min(TOPK_MAX, seq // 4)
    topk_s = min(TOPK_MAX, (n_pages * page + dec_seq) // 4)
    t_p = bsz * seq

    xp = x_prompt.reshape(t_p, d)
    xs = x_sample.reshape(n_dec, d)
    outs = [[] for _ in range(8)]
    for i in range(depth):
        wn, wt = _pack_w_in(w_in[i])
        gqb = jnp.broadcast_to((jnp.tile(g_q[i], N_HEADS) * (ATTN_SCALE * LOG2E))[:, None], (1024, QBLK))
        gk = jnp.tile(g_k[i], KV_HEADS)[None]
        gmix = g_mix[i][None]
        gsgu = g_sgu[i][None]
        fin_w = (w_o[i].astype(BF16), g_ffn[i][None], w_up[i].astype(BF16), w_down[i].astype(BF16),
                 g_ple[i][None], w_pg[i].astype(BF16), w_p[i].astype(BF16))

        (qt, kf, kb, vf, vt, qi3t, kif, ki3, wit, vn, sga, sgu) = _proj(
            xp, wn, wt, gmix, gqb, gk, gsgu, w_s[i], jnp.transpose(b_s[i]), sample=False, tm=256)
        oatt = _attn_prompt(qt, qi3t, wit, kb.reshape(bsz, seq, 256), vt, ki3.reshape(bsz, seq, 256),
                            bsz=bsz, seq=seq, topk=topk_p)
        xp = _finish(xp, oatt.reshape(t_p, 1024), sga, sgu, p_prompt[i].reshape(t_p, -1), *fin_w,
                     tm=256, name="finish_prompt")
        outs[0].append(kf.reshape(bsz, seq, KV_HEADS, HEAD_DIM))
        outs[1].append(vf.reshape(bsz, seq, KV_HEADS, HEAD_DIM))
        outs[2].append(kif.reshape(bsz, seq, IDX_DIM))
        outs[3].append(vn.reshape(bsz, seq, d))

        w00 = jnp.repeat(w_s[i][:, 0, 0], CHUNK)[None]
        b0 = jnp.repeat(b_s[i][:, 0], CHUNK)[None]
        (qt, kf, kb, vf, vt, qi3t, kif, ki3, wit, vn, sga, sgu) = _proj(
            xs, wn, wt, gmix, gqb, gk, gsgu, w00, b0, sample=True, tm=n_dec)
        oatt = _attn_decode(page_table, qt, qi3t, wit, kf, vf, kif, cache_k[i], cache_v[i], cache_kidx[i],
                            topk=topk_s)
        xs = _finish(xs, oatt, sga, sgu, p_sample[i].reshape(n_dec, -1), *fin_w, tm=n_dec,
                     name="finish_sample")
        outs[4].append(kf.reshape(n_dec, 1, KV_HEADS, HEAD_DIM))
        outs[5].append(vf.reshape(n_dec, 1, KV_HEADS, HEAD_DIM))
        outs[6].append(kif.reshape(n_dec, 1, IDX_DIM))
        outs[7].append(vn.reshape(n_dec, 1, d))

    return (xp.reshape(bsz, seq, d), xs.reshape(n_dec, 1, d), *[jnp.stack(o) for o in outs])
```

```python
import functools
import math

import jax
import jax.numpy as jnp
from jax import lax
from jax.experimental import pallas as pl
from jax.experimental.pallas import tpu as pltpu

F32 = jnp.float32
BF16 = jnp.bfloat16
I32 = jnp.int32

N_HEADS = 16
HEAD_DIM = 64
HEAD_SHIFT = 6
KV_HEADS = 4
Q_PER_KV = N_HEADS // KV_HEADS
IDX_HEADS = 8
IDX_DIM = 64
TOPK_MAX = 256
QBLK = 128
CHUNK = 128
SGU_GROUPS = 8
EPS = 1e-6
LOG2E = math.log2(math.e)
ATTN_SCALE = HEAD_DIM ** -0.5
IDX_SCALE = IDX_DIM ** -0.5
IDX_W_SCALE = IDX_HEADS ** -0.5

LANES = 128
SUBLANES = 8
INT_MAX = 2 ** 31 - 1
KEY_NEG_INF = -(2 ** 31) + 0x7FFFFF
KEY_POS_INF = 0x7F800000
KEY_MIN_NORMAL = 0x00800000
F32_LOWEST = -3.4028234663852886e38
MASK_BIAS = -1e30
M_INIT = -1e29
VMEM_LIMIT = 56 * 1024 * 1024
SEL_ROWS = 512
SEL_GROUPS = 256
SEL_STEPS_UNCHECKED = 9
SEL_STEPS_PER_CHECK = 2
HINT_DROP = 5
DEC_SCORE_GROUP = 4
DEC_ATTN_GROUP = 2
TIE_ROWS = 256
ONES_ROWS = 16
STEP_SPLIT = 1


def _dot(a, b):
    return jnp.dot(a, b, preferred_element_type=F32)


def _dot_t(a, b):
    return lax.dot_general(a, b, (((1,), (1,)), ((), ())), preferred_element_type=F32)


def _hi_lo(x):
    hi = x.astype(BF16)
    lo = (x - hi.astype(F32)).astype(BF16)
    return hi, lo


def _sigmoid(x):
    return 1.0 / (1.0 + jnp.exp(-x))


def _rms_rows(x, g):
    return x * lax.rsqrt(jnp.mean(x * x, axis=-1, keepdims=True) + EPS) * g


def _head_rms_scale(z):
    w = z.shape[1]
    seg = lax.broadcasted_iota(I32, (w, LANES), 0) >> HEAD_SHIFT
    col = lax.broadcasted_iota(I32, (w, LANES), 1)
    g = (seg == col).astype(BF16)
    row_t = lax.broadcasted_iota(I32, (LANES, w), 0)
    seg_t = lax.broadcasted_iota(I32, (LANES, w), 1) >> HEAD_SHIFT
    g_t = (row_t == seg_t).astype(BF16)
    hi, lo = _hi_lo(z * z)
    ssq = _dot(hi, g) + _dot(lo, g)
    r = lax.rsqrt(ssq * (1.0 / HEAD_DIM) + EPS)
    rh, rl = _hi_lo(r)
    return _dot(rh, g_t) + _dot(rl, g_t)


def _to_key(x):
    b = lax.bitcast_convert_type(x, I32)
    return b ^ ((b >> 31) & INT_MAX)


def _fold8(x, op):
    parts = [x[r * SUBLANES:(r + 1) * SUBLANES] for r in range(x.shape[0] // SUBLANES)]
    while len(parts) > 1:
        nxt = [op(parts[a], parts[a + 1]) for a in range(0, len(parts) - 1, 2)]
        if len(parts) % 2:
            nxt.append(parts[-1])
        parts = nxt
    return parts[0]


def _col_reduce8(x8, op):
    x8 = op(x8, pltpu.roll(x8, 4, 0))
    x8 = op(x8, pltpu.roll(x8, 2, 0))
    x8 = op(x8, pltpu.roll(x8, 1, 0))
    return x8[0:1]


_N_K, _N_V, _N_KI, _N_U, _N_VB, _N_GA, _N_GB, _N_END = 0, 256, 512, 640, 1664, 2688, 3712, 4736
_T_Q, _T_QI, _T_WI, _T_END = 0, 1024, 1536, 1552


def _proj_kernel(x_ref, wn_ref, wt_ref, gmix_ref, gqb_ref, gk_ref, gsgu_ref, ws_ref, bs_ref,
                 qt_ref, kf_ref, kb_ref, vf_ref, vt_ref, qi3t_ref, kif_ref, ki3_ref, wit_ref,
                 vn_ref, sga_ref, sgu_ref, *, sample):
    tm = x_ref.shape[0]
    nq = tm // QBLK
    xn = _rms_rows(x_ref[...], gmix_ref[...]).astype(BF16)

    def nat(lo, hi):
        return _dot(xn, wn_ref[:, lo:hi])

    def tra(lo, hi):
        return _dot_t(wt_ref[lo:hi, :], xn)

    z3 = tra(_T_Q, _T_QI).reshape(N_HEADS, HEAD_DIM, tm)
    r = lax.rsqrt(jnp.mean(z3 * z3, axis=1, keepdims=True) + EPS)
    qn = (z3 * r).reshape(N_HEADS * HEAD_DIM, tm)
    for n in range(nq):
        qt_ref[n] = (qn[:, n * QBLK:(n + 1) * QBLK] * gqb_ref[...]).astype(BF16)

    zqi = tra(_T_QI, _T_WI) * IDX_SCALE
    zero = jnp.zeros((IDX_DIM, tm), BF16)
    for h in range(IDX_HEADS):
        hq, lq = _hi_lo(zqi[h * IDX_DIM:(h + 1) * IDX_DIM])
        blk = jnp.concatenate([hq, lq, hq, zero], axis=0)
        for n in range(nq):
            qi3t_ref[n, :, h * QBLK:(h + 1) * QBLK] = blk[:, n * QBLK:(n + 1) * QBLK]
    zwi = tra(_T_WI, _T_END) * IDX_W_SCALE
    for n in range(nq):
        wit_ref[n] = zwi[0:IDX_HEADS, n * QBLK:(n + 1) * QBLK]

    zk = nat(_N_K, _N_V)
    kn = zk * _head_rms_scale(zk) * gk_ref[...]
    kb_ref[...] = kn.astype(BF16)
    zv = nat(_N_V, _N_KI)
    vt_ref[...] = zv.T.astype(BF16)
    for c in range(KV_HEADS):
        kf_ref[:, c, :] = kn[:, c * HEAD_DIM:(c + 1) * HEAD_DIM]
        vf_ref[:, c, :] = zv[:, c * HEAD_DIM:(c + 1) * HEAD_DIM]
    first = lax.broadcasted_iota(I32, (tm, LANES), 1) < IDX_DIM
    zki = nat(_N_KI, _N_U)
    kif_ref[...] = zki[:, 0:IDX_DIM]
    hk, lk = _hi_lo(zki)
    ki3_ref[:, 0:LANES] = hk
    ki3_ref[:, LANES:2 * LANES] = jnp.where(first, lk, jnp.zeros_like(lk))

    u = jax.nn.gelu(nat(_N_U, _N_VB))
    vn = _rms_rows(jax.nn.gelu(nat(_N_VB, _N_GA)), gsgu_ref[...])
    vn_ref[...] = vn
    sga_ref[...] = _sigmoid(nat(_N_GA, _N_GB))
    sgb = _sigmoid(nat(_N_GB, _N_END))
    if sample:
        sgu_ref[...] = sgb * (u * (vn * ws_ref[...] + bs_ref[...]))
    else:
        tril = (lax.broadcasted_iota(I32, (CHUNK, CHUNK), 0)
                >= lax.broadcasted_iota(I32, (CHUNK, CHUNK), 1))
        for g in range(SGU_GROUPS):
            wg = jnp.where(tril, ws_ref[g], 0.0).astype(BF16)
            bg = bs_ref[:, g:g + 1]
            cols = slice(g * CHUNK, (g + 1) * CHUNK)
            chunks = [vn[n * CHUNK:(n + 1) * CHUNK, cols].astype(BF16) for n in range(tm // CHUNK)]
            mixed = _dot(wg, jnp.concatenate(chunks, axis=1))
            for n in range(tm // CHUNK):
                rows = slice(n * CHUNK, (n + 1) * CHUNK)
                sgu_ref[rows, cols] = sgb[rows, cols] * (u[rows, cols] * (mixed[:, rows] + bg))


def _proj(x, wn, wt, gmix, gqb, gk, gsgu, ws, bs, *, sample, tm):
    t = x.shape[0]
    d = x.shape[1]
    nq = tm // QBLK
    row = lambda i: (i, 0)
    const2 = lambda i: (0, 0)
    blk3 = lambda i: (i, 0, 0)
    if sample:
        ws_spec = pl.BlockSpec(ws.shape, const2)
    else:
        ws_spec = pl.BlockSpec(ws.shape, lambda i: (0, 0, 0))
    bs_spec = pl.BlockSpec(bs.shape, const2)
    out_shapes = (
        jax.ShapeDtypeStruct((t // QBLK, 1024, QBLK), BF16),
        jax.ShapeDtypeStruct((t, KV_HEADS, HEAD_DIM), F32), jax.ShapeDtypeStruct((t, 256), BF16),
        jax.ShapeDtypeStruct((t, KV_HEADS, HEAD_DIM), F32), jax.ShapeDtypeStruct((256, t), BF16),
        jax.ShapeDtypeStruct((t // QBLK, 256, IDX_HEADS * QBLK), BF16),
        jax.ShapeDtypeStruct((t, IDX_DIM), F32),
        jax.ShapeDtypeStruct((t, 256), BF16),
        jax.ShapeDtypeStruct((t // QBLK, IDX_HEADS, QBLK), F32),
        jax.ShapeDtypeStruct((t, 1024), F32),
        jax.ShapeDtypeStruct((t, 1024), F32),
        jax.ShapeDtypeStruct((t, 1024), F32),
    )
    out_specs = (
        pl.BlockSpec((nq, 1024, QBLK), blk3),
        pl.BlockSpec((tm, KV_HEADS, HEAD_DIM), blk3), pl.BlockSpec((tm, 256), row),
        pl.BlockSpec((tm, KV_HEADS, HEAD_DIM), blk3), pl.BlockSpec((256, tm), lambda i: (0, i)),
        pl.BlockSpec((nq, 256, IDX_HEADS * QBLK), blk3),
        pl.BlockSpec((tm, IDX_DIM), row),
        pl.BlockSpec((tm, 256), row),
        pl.BlockSpec((nq, IDX_HEADS, QBLK), blk3),
        pl.BlockSpec((tm, 1024), row),
        pl.BlockSpec((tm, 1024), row),
        pl.BlockSpec((tm, 1024), row),
    )
    return pl.pallas_call(
        functools.partial(_proj_kernel, sample=sample),
        grid=(t // tm,),
        in_specs=[
            pl.BlockSpec((tm, d), row),
            pl.BlockSpec(wn.shape, const2, pipeline_mode=pl.Buffered(1)),
            pl.BlockSpec(wt.shape, const2, pipeline_mode=pl.Buffered(1)),
            pl.BlockSpec((1, d), const2), pl.BlockSpec(gqb.shape, const2),
            pl.BlockSpec((1, 256), const2), pl.BlockSpec((1, 1024), const2),
            ws_spec, bs_spec,
        ],
        out_specs=out_specs,
        out_shape=out_shapes,
        compiler_params=pltpu.CompilerParams(
            dimension_semantics=("arbitrary",), vmem_limit_bytes=VMEM_LIMIT),
        name="proj_sample" if sample else "proj_prompt",
    )(x, wn, wt, gmix, gqb, gk, gsgu, ws, bs)


def _key_to_f32(key):
    bits = key ^ ((key >> 31) & INT_MAX)
    bits = jnp.where(jnp.logical_and(bits > 0, bits < KEY_MIN_NORMAL), KEY_MIN_NORMAL, bits)
    return lax.bitcast_convert_type(bits, F32)


def _select(score_ref, n_steps, topk, lo_hint, hi_hint):
    lanes = score_ref.shape[1]
    topk_f = float(topk)

    def count(pred):
        def body(t, acc):
            start = pl.multiple_of(t * SEL_ROWS, SEL_ROWS)
            tile = score_ref[pl.ds(start, SEL_ROWS), :]
            return acc + _fold8(jnp.where(pred(tile), 1.0, 0.0), jnp.add)
        acc = lax.fori_loop(0, n_steps, body, jnp.zeros((SUBLANES, lanes), F32))
        return _col_reduce8(acc, jnp.add)

    def update(st, mid):
        lo, hi, cnt_lo, cnt_hi = st
        thr_b = jnp.broadcast_to(_key_to_f32(mid), (SEL_ROWS, lanes))
        c = count(lambda tile: tile >= thr_b)
        ge = c >= topk_f
        lo2 = jnp.where(ge, mid, lo)
        hi2 = jnp.where(c == topk_f, mid + 1, jnp.where(ge, hi, mid))
        hi2 = jnp.where(jnp.logical_and(lo2 == 0, hi2 == KEY_MIN_NORMAL), 1, hi2)
        return lo2, hi2, jnp.where(ge, c, cnt_lo), jnp.where(ge, cnt_hi, c)

    def pivot(lo, hi):
        mid = (lo >> 1) + (hi >> 1) + (lo & hi & 1)
        mid = jnp.where(jnp.logical_and(lo < 0, hi > 0), 0, mid)
        return jnp.where(jnp.logical_and(lo < KEY_MIN_NORMAL, hi > KEY_MIN_NORMAL), KEY_MIN_NORMAL, mid)

    def cond(st):
        return jnp.logical_and(st[4] > 0.5, st[5] < 48 // SEL_STEPS_PER_CHECK)

    def body(st):
        lo, hi, cnt_lo, cnt_hi = lax.fori_loop(
            0, SEL_STEPS_PER_CHECK, lambda _, s: update(s, pivot(s[0], s[1])), st[:4])
        active = jnp.where(hi != lo + 1, 1.0, 0.0)
        return lo, hi, cnt_lo, cnt_hi, jnp.max(active), st[5] + 1

    st = (jnp.full((1, lanes), KEY_NEG_INF, I32), jnp.full((1, lanes), KEY_POS_INF + 1, I32),
          jnp.full((1, lanes), 2.0 * topk_f, F32), jnp.zeros((1, lanes), F32))
    scaled = jnp.where(hi_hint > (HINT_DROP << 23) + KEY_MIN_NORMAL, hi_hint - (HINT_DROP << 23), KEY_NEG_INF)
    for hint in (lo_hint, hi_hint, scaled):
        inside = jnp.logical_and(hint > st[0], hint < st[1])
        st = update(st, jnp.where(inside, hint, pivot(st[0], st[1])))
    st = lax.fori_loop(0, SEL_STEPS_UNCHECKED, lambda _, s: update(s, pivot(s[0], s[1])), st)
    lo, _, cnt_lo, cnt_hi, _, _ = lax.while_loop(cond, body, st + (jnp.float32(1.0), jnp.int32(0)))

    tie = jnp.logical_and(cnt_lo > topk_f, lo > KEY_NEG_INF)
    any_tie = jnp.max(jnp.where(tie, 1.0, 0.0))

    @pl.when(any_tie > 0.5)
    def _():
        tri = (lax.broadcasted_iota(I32, (TIE_ROWS, TIE_ROWS), 1)
               <= lax.broadcasted_iota(I32, (TIE_ROWS, TIE_ROWS), 0)).astype(BF16)
        at_b = jnp.broadcast_to(_key_to_f32(lo), (TIE_ROWS, lanes))
        next_b = jnp.broadcast_to(_key_to_f32(lo + 1), (TIE_ROWS, lanes))
        tie_b = jnp.broadcast_to(tie, (TIE_ROWS, lanes))
        need_b = jnp.broadcast_to(topk_f - cnt_hi, (TIE_ROWS, lanes))

        def resolve(t, seen):
            for part in range(SEL_ROWS // TIE_ROWS):
                start = pl.multiple_of(t * SEL_ROWS + part * TIE_ROWS, TIE_ROWS)
                tile = score_ref[pl.ds(start, TIE_ROWS), :]
                tied = jnp.logical_and(jnp.logical_and(tile >= at_b, tile < next_b), tie_b)
                ones = jnp.where(tied, 1.0, 0.0)
                upto = _dot(tri, ones.astype(BF16)) + seen
                score_ref[pl.ds(start, TIE_ROWS), :] = jnp.where(jnp.logical_and(tied, upto > need_b), -jnp.inf, tile)
                seen = seen + _col_reduce8(_fold8(ones, jnp.add), jnp.add)
            return seen

        lax.fori_loop(0, n_steps, resolve, jnp.zeros((1, lanes), F32))

    return jnp.maximum(_key_to_f32(lo), F32_LOWEST)


def _group_hints(gm):
    lo_hint = _to_key(_col_reduce8(_fold8(gm, jnp.minimum), jnp.minimum))
    hi_hint = _to_key(_col_reduce8(_fold8(gm, jnp.maximum), jnp.maximum)) + 1
    return lo_hint, hi_hint


def _attn_kernel(qt_ref, qi3t_ref, wit_ref, k_ref, vt_ref, ki3_ref, o_ref,
                 key_s, gm_s, wq_s, m_s, acc_s, sa_s, sb_s, la_s, lb_s, *, topk, ck):
    i = pl.program_id(1)
    n_ck = (i * QBLK + QBLK - 1) // ck + 1
    rs = 64
    gq = Q_PER_KV * QBLK

    wq_s[...] = jnp.zeros(wq_s.shape, BF16)
    for c in range(KV_HEADS):
        for g in range(Q_PER_KV):
            h = c * Q_PER_KV + g
            wq_s[c, c * HEAD_DIM:(c + 1) * HEAD_DIM, g * QBLK:(g + 1) * QBLK] = (
                qt_ref[h * HEAD_DIM:(h + 1) * HEAD_DIM, :])
    m_s[...] = jnp.full(m_s.shape, M_INIT, F32)
    acc_s[...] = jnp.zeros(acc_s.shape, F32)
    gm_s[...] = jnp.full(gm_s.shape, -jnp.inf, F32)

    qpos = lax.broadcasted_iota(I32, (rs, QBLK), 1) + i * QBLK
    krow = lax.broadcasted_iota(I32, (rs, QBLK), 0)
    hk = ck // 2
    seq = k_ref.shape[0]

    def idx_dot(row0, buf):
        buf[...] = _dot(ki3_ref[pl.ds(pl.multiple_of(row0, hk), hk), :], qi3t_ref[...])

    def to_keys(row0, buf):
        for p in range(hk // rs):
            acc = jnp.zeros((rs, QBLK), F32)
            for h in range(IDX_HEADS):
                acc = acc + jnp.maximum(buf[p * rs:(p + 1) * rs, h * QBLK:(h + 1) * QBLK], 0.0) * wit_ref[h:h + 1, :]
            sc = jnp.where(krow + (row0 + p * rs) <= qpos, acc, -jnp.inf)
            key_s[pl.ds(pl.multiple_of(row0 + p * rs, rs), rs), :] = sc
            gm_s[p * rs:(p + 1) * rs, :] = jnp.maximum(gm_s[p * rs:(p + 1) * rs, :], sc)

    idx_dot(0, la_s)

    def score_chunk(j, j_next):
        base = j * ck
        idx_dot(base + hk, lb_s)
        to_keys(base, la_s)
        idx_dot(j_next * ck, la_s)
        to_keys(base + hk, lb_s)

    def score_pair(t, carry):
        score_chunk(2 * t, 2 * t + 1)
        score_chunk(2 * t + 1, jnp.minimum(2 * t + 2, seq // ck - 1))
        return carry

    lax.fori_loop(0, n_ck // 2, score_pair, 0)

    @pl.when(n_ck % 2 == 1)
    def _():
        score_chunk(n_ck - 1, n_ck - 1)

    lo_hint, hi_hint = _group_hints(gm_s[...])
    tau = jnp.broadcast_to(_select(key_s, n_ck * (ck // SEL_ROWS), topk, lo_hint, hi_hint), (ck, QBLK))

    s_bufs = (sa_s, sb_s)
    last_ck = seq // ck - 1
    ones_rows = jnp.ones((ONES_ROWS, ck), BF16)

    n_step = KV_HEADS * STEP_SPLIT
    sw = gq // STEP_SPLIT

    def qk(j, t, buf):
        c, part = divmod(t, STEP_SPLIT)
        base = pl.multiple_of(j * ck, ck)
        buf[...] = _dot(k_ref[pl.ds(base, ck), :], wq_s[c, :, part * sw:(part + 1) * sw])

    def softmax_pv(base, t, buf, bias):
        c, part = divmod(t, STEP_SPLIT)
        cols = slice(part * sw, (part + 1) * sw)
        s = buf[...] + bias
        m_old = m_s[c, :, cols]
        m_new = jnp.maximum(m_old, jnp.max(s, axis=0, keepdims=True))
        p = jnp.exp2(s - m_new).astype(BF16)
        alpha = jnp.exp2(m_old - m_new)
        m_s[c, :, cols] = m_new
        vc = jnp.concatenate([vt_ref[c * HEAD_DIM:(c + 1) * HEAD_DIM, pl.ds(base, ck)], ones_rows], axis=0)
        acc_s[c, :, cols] = alpha * acc_s[c, :, cols] + _dot(vc, p)

    qk(0, 0, s_bufs[0])

    def attn_chunk(j, j_next):
        base = pl.multiple_of(j * ck, ck)
        bias = jnp.where(key_s[pl.ds(base, ck), :] >= tau, 0.0, MASK_BIAS)
        bias = jnp.concatenate([bias] * (sw // QBLK), axis=1)
        for t in range(n_step):
            if t + 1 < n_step:
                qk(j, t + 1, s_bufs[(t + 1) % 2])
            else:
                qk(j_next, 0, s_bufs[0])
            softmax_pv(base, t, s_bufs[t % 2], bias)

    def attn_pair(t, carry):
        attn_chunk(2 * t, 2 * t + 1)
        attn_chunk(2 * t + 1, jnp.minimum(2 * t + 2, last_ck))
        return carry

    lax.fori_loop(0, n_ck // 2, attn_pair, 0)

    @pl.when(n_ck % 2 == 1)
    def _():
        attn_chunk(n_ck - 1, n_ck - 1)

    for hp in range(N_HEADS // 2):
        parts = []
        for h in (2 * hp, 2 * hp + 1):
            c, g = divmod(h, Q_PER_KV)
            cols = slice(g * QBLK, (g + 1) * QBLK)
            parts.append(acc_s[c, 0:HEAD_DIM, cols] / acc_s[c, HEAD_DIM:HEAD_DIM + 1, cols])
        o_ref[:, hp * LANES:(hp + 1) * LANES] = jnp.concatenate(parts, axis=0).T


def _attn_prompt(qt, qi3t, wit, kb, vt, ki3, *, bsz, seq, topk, ck=512):
    nblk = seq // QBLK
    grid = (bsz, nblk)
    qblk = lambda bi, i: (bi * nblk + i, 0, 0)
    per_seq = lambda bi, i: (bi, 0, 0)
    gq = Q_PER_KV * QBLK
    return pl.pallas_call(
        functools.partial(_attn_kernel, topk=topk, ck=ck),
        grid=grid,
        in_specs=[
            pl.BlockSpec((None, 1024, QBLK), qblk),
            pl.BlockSpec((None, 256, IDX_HEADS * QBLK), qblk),
            pl.BlockSpec((None, IDX_HEADS, QBLK), qblk),
            pl.BlockSpec((None, seq, 256), per_seq, pipeline_mode=pl.Buffered(1)),
            pl.BlockSpec((256, seq), lambda bi, i: (0, bi), pipeline_mode=pl.Buffered(1)),
            pl.BlockSpec((None, seq, 256), per_seq, pipeline_mode=pl.Buffered(1)),
        ],
        out_specs=pl.BlockSpec((None, QBLK, 1024), lambda bi, i: (bi, i, 0)),
        out_shape=jax.ShapeDtypeStruct((bsz, seq, 1024), F32),
        scratch_shapes=[
            pltpu.VMEM((seq, QBLK), F32),
            pltpu.VMEM((SEL_GROUPS, QBLK), F32),
            pltpu.VMEM((KV_HEADS, 256, gq), BF16),
            pltpu.VMEM((KV_HEADS, 1, gq), F32),
            pltpu.VMEM((KV_HEADS, HEAD_DIM + ONES_ROWS, gq), F32),
            pltpu.VMEM((ck, gq // STEP_SPLIT), F32), pltpu.VMEM((ck, gq // STEP_SPLIT), F32),
            pltpu.VMEM((ck // 2, IDX_HEADS * QBLK), F32), pltpu.VMEM((ck // 2, IDX_HEADS * QBLK), F32),
        ],
        compiler_params=pltpu.CompilerParams(
            dimension_semantics=("arbitrary", "arbitrary"), vmem_limit_bytes=VMEM_LIMIT),
        name="attn_prompt",
    )(qt, qi3t, wit, kb, vt, ki3)


def _dec_score_kernel(pt_ref, qi3_ref, wcol_ref, kis_ref, *rest, n_pages, page, group):
    key_ref = rest[group * n_pages]
    del pt_ref
    n_keys = key_ref.shape[2]
    n_past = n_pages * page
    kpos = lax.broadcasted_iota(I32, (1, n_keys), 1)
    for g in range(group):
        page_refs = rest[g * n_pages:(g + 1) * n_pages]
        ki_all = jnp.concatenate(
            [r[...] for r in page_refs] + [jnp.broadcast_to(kis_ref[g], (IDX_DIM, n_keys - n_past))], axis=1)
        hk, lk = _hi_lo(ki_all)
        q3 = qi3_ref[g]
        qh = q3[:, 0:IDX_DIM]
        ql = q3[:, IDX_DIM:2 * IDX_DIM]
        lg = _dot(qh, hk) + _dot(ql, hk) + _dot(qh, lk)
        score = jnp.sum(jnp.maximum(lg, 0.0) * wcol_ref[g], axis=0, keepdims=True) + 0.0
        key_ref[g] = jnp.where(kpos <= n_past, score, -jnp.inf)


def _dec_select_kernel(key_ref, tau_ref, keyout_ref, *, topk):
    keyout_ref[...] = key_ref[...]
    gm = key_ref[0:SEL_GROUPS, :]
    for t in range(1, key_ref.shape[0] // SEL_GROUPS):
        gm = jnp.maximum(gm, key_ref[t * SEL_GROUPS:(t + 1) * SEL_GROUPS, :])
    lo_hint, hi_hint = _group_hints(gm)
    tau_ref[...] = _select(keyout_ref, key_ref.shape[0] // SEL_ROWS, topk, lo_hint, hi_hint)


def _dec_attn_kernel(pt_ref, lhs_ref, key_ref, tau_ref, ks_ref, vs_ref, *rest, n_pages, page, group):
    o_ref = rest[2 * group * n_pages]
    del pt_ref
    n_past = n_pages * page
    r_i = lax.broadcasted_iota(I32, (N_HEADS, KV_HEADS * HEAD_DIM), 0)
    l_i = lax.broadcasted_iota(I32, (N_HEADS, KV_HEADS * HEAD_DIM), 1)
    own_lanes = (l_i >> HEAD_SHIFT) == (r_i >> 2)
    for g in range(group):
        k_refs = rest[g * n_pages:(g + 1) * n_pages]
        v_refs = rest[(group + g) * n_pages:(group + g + 1) * n_pages]
        lhs = lhs_ref[g]
        bias = jnp.where(key_ref[g] >= tau_ref[g], 0.0, MASK_BIAS)
        s = jnp.concatenate([_dot(lhs, r[...].astype(BF16)) for r in k_refs], axis=1)
        s = s + bias[:, 0:n_past]
        ks = ks_ref[g].astype(BF16).astype(F32)
        s_self = jnp.sum(lhs.astype(F32) * ks, axis=1, keepdims=True) + bias[:, n_past:n_past + 1]
        m = jnp.maximum(jnp.max(s, axis=1, keepdims=True), s_self)
        p = jnp.exp2(s - m)
        p_self = jnp.exp2(s_self - m)
        l = jnp.sum(p, axis=1, keepdims=True) + p_self
        vt_all = jnp.concatenate([r[...].astype(BF16) for r in v_refs], axis=1)
        vs = vs_ref[g].astype(BF16).astype(F32)
        o = (_dot_t(p.astype(BF16), vt_all) + p_self.astype(BF16).astype(F32) * vs) / l
        o = jnp.where(own_lanes, o, 0.0)
        o_ref[g] = (o[:, 0:HEAD_DIM] + o[:, HEAD_DIM:2 * HEAD_DIM]
                    + o[:, 2 * HEAD_DIM:3 * HEAD_DIM] + o[:, 3 * HEAD_DIM:4 * HEAD_DIM])


def _attn_decode(page_table, qt_s, qi3t_s, wit_s, kf_s, vf_s, kif_s, cache_k, cache_v, cache_kidx, *, topk):
    n, n_pages = page_table.shape
    n_phys, page = cache_k.shape[0], cache_k.shape[1]
    n_keys = -(-(n_pages * page + 1) // SEL_ROWS) * SEL_ROWS
    ckt = jnp.transpose(cache_k, (0, 2, 3, 1)).reshape(n_phys, KV_HEADS * HEAD_DIM, page)
    cvt = jnp.transpose(cache_v, (0, 2, 3, 1)).reshape(n_phys, KV_HEADS * HEAD_DIM, page)
    ckit = jnp.transpose(cache_kidx, (0, 2, 1))

    def page_specs(rows, group):
        return [pl.BlockSpec((None, rows, page), lambda bi, pt, g=g, p=p: (pt[bi * group + g, p], 0, 0))
                for g in range(group) for p in range(n_pages)]

    per_step = lambda bi, *_: (bi, 0, 0)
    gs, ga = DEC_SCORE_GROUP, DEC_ATTN_GROUP
    assert n % gs == 0 and n % ga == 0
    q_s = jnp.transpose(qt_s[0])
    qi3 = jnp.transpose(qi3t_s[0].reshape(256, IDX_HEADS, n), (2, 1, 0))
    wcol = jnp.transpose(wit_s[0])[:, :, None]
    qh = q_s.reshape(n, N_HEADS, 1, HEAD_DIM)
    c_of_head = (jnp.arange(N_HEADS) // Q_PER_KV)[None, :, None, None]
    lhs = jnp.where(c_of_head == jnp.arange(KV_HEADS)[None, None, :, None], qh,
                    jnp.zeros_like(qh)).reshape(n, N_HEADS, KV_HEADS * HEAD_DIM)

    keys = pl.pallas_call(
        functools.partial(_dec_score_kernel, n_pages=n_pages, page=page, group=gs),
        grid_spec=pltpu.PrefetchScalarGridSpec(
            num_scalar_prefetch=1, grid=(n // gs,),
            in_specs=[pl.BlockSpec((gs, IDX_HEADS, 256), per_step),
                      pl.BlockSpec((gs, IDX_HEADS, 1), per_step),
                      pl.BlockSpec((gs, IDX_DIM, 1), per_step)]
            + page_specs(IDX_DIM, gs),
            out_specs=pl.BlockSpec((gs, 1, n_keys), per_step)),
        out_shape=jax.ShapeDtypeStruct((n, 1, n_keys), F32),
        compiler_params=pltpu.CompilerParams(dimension_semantics=("arbitrary",), vmem_limit_bytes=VMEM_LIMIT),
        name="dec_score",
    )(page_table, qi3, wcol, kif_s.reshape(n, IDX_DIM, 1), *([ckit] * (gs * n_pages)))

    tau, keys_t = pl.pallas_call(
        functools.partial(_dec_select_kernel, topk=topk),
        out_shape=(jax.ShapeDtypeStruct((1, n), F32), jax.ShapeDtypeStruct((n_keys, n), F32)),
        name="dec_select",
    )(jnp.transpose(keys.reshape(n, n_keys)))
    keys = jnp.transpose(keys_t).reshape(n, 1, n_keys)

    o = pl.pallas_call(
        functools.partial(_dec_attn_kernel, n_pages=n_pages, page=page, group=ga),
        grid_spec=pltpu.PrefetchScalarGridSpec(
            num_scalar_prefetch=1, grid=(n // ga,),
            in_specs=[pl.BlockSpec((ga, N_HEADS, 256), per_step),
                      pl.BlockSpec((ga, 1, n_keys), per_step),
                      pl.BlockSpec((ga, 1, 1), per_step),
                      pl.BlockSpec((ga, 1, 256), per_step),
                      pl.BlockSpec((ga, 1, 256), per_step)]
            + page_specs(KV_HEADS * HEAD_DIM, ga) + page_specs(KV_HEADS * HEAD_DIM, ga),
            out_specs=pl.BlockSpec((ga, N_HEADS, HEAD_DIM), per_step)),
        out_shape=jax.ShapeDtypeStruct((n, N_HEADS, HEAD_DIM), F32),
        compiler_params=pltpu.CompilerParams(dimension_semantics=("arbitrary",), vmem_limit_bytes=VMEM_LIMIT),
        name="dec_attn",
    )(page_table, lhs, keys, tau.reshape(n, 1, 1), kf_s.reshape(n, 1, 256), vf_s.reshape(n, 1, 256),
      *([ckt] * (ga * n_pages)), *([cvt] * (ga * n_pages)))
    return o.reshape(n, N_HEADS * HEAD_DIM)


def _finish_kernel(x_ref, oatt_ref, sga_ref, sgu_ref, p_ref, wo_ref, gffn_ref, wup_ref, wdn_ref,
                   gple_ref, wpg_ref, wp_ref, y_ref):
    merged = (sga_ref[...] * oatt_ref[...] + sgu_ref[...]).astype(BF16)
    x = x_ref[...] + _dot(merged, wo_ref[...])
    hf = _rms_rows(x, gffn_ref[...]).astype(BF16)
    up = jnp.maximum(_dot(hf, wup_ref[...]), 0.0)
    x = x + _dot((up * up).astype(BF16), wdn_ref[...])
    hp = _rms_rows(x, gple_ref[...]).astype(BF16)
    gate = _sigmoid(_dot(hp, wpg_ref[...]))
    y_ref[...] = x + gate * _dot(p_ref[...].astype(BF16), wp_ref[...])


def _finish(x, oatt, sga, sgu, p, wo, gffn, wup, wdn, gple, wpg, wp, *, tm, name):
    t, d = x.shape
    row = lambda i: (i, 0)
    const = lambda i: (0, 0)

    def wspec(w):
        return pl.BlockSpec(w.shape, const, pipeline_mode=pl.Buffered(1))

    return pl.pallas_call(
        _finish_kernel,
        grid=(t // tm,),
        in_specs=[pl.BlockSpec((tm, d), row), pl.BlockSpec((tm, d), row), pl.BlockSpec((tm, d), row),
                  pl.BlockSpec((tm, d), row), pl.BlockSpec((tm, p.shape[1]), row),
                  wspec(wo), pl.BlockSpec((1, d), const), wspec(wup), wspec(wdn),
                  pl.BlockSpec((1, d), const), wspec(wpg), wspec(wp)],
        out_specs=pl.BlockSpec((tm, d), row),
        out_shape=jax.ShapeDtypeStruct((t, d), F32),
        compiler_params=pltpu.CompilerParams(
            dimension_semantics=("arbitrary",), vmem_limit_bytes=VMEM_LIMIT),
        name=name,
    )(x, oatt, sga, sgu, p, wo, gffn, wup, wdn, gple, wpg, wp)


def _pack_w_in(w_in):
    d = w_in.shape[0]
    splits = (1024, 256, 256, 512, 64, 8, 1024, 1024, 1024, 1024)
    offs = [0]
    for n in splits:
        offs.append(offs[-1] + n)
    wq, wk, wv, wqi, wki, wwi, wu, wvb, wga, wgb = [w_in[:, offs[i]:offs[i + 1]] for i in range(10)]
    wn = jnp.concatenate([wk, wv, wki, wki, wu, wvb, wga, wgb], axis=1).astype(BF16)
    wt = jnp.concatenate([wq, wqi, wwi, jnp.zeros((d, _T_END - _T_WI - IDX_HEADS), w_in.dtype)],
                         axis=1).T.astype(BF16)
    return wn, wt


def kernel(x_prompt, x_sample, cache_k, cache_v, cache_kidx, page_table, p_prompt, p_sample, g_mix, w_in, g_q, g_k, g_sgu, w_s, b_s, w_o, g_ffn, w_up, w_down, g_ple, w_pg, w_p):
    depth = w_in.shape[0]
    bsz, seq, d = x_prompt.shape
    n_dec, dec_seq, _ = x_sample.shape
    assert dec_seq == 1 and seq % 512 == 0
    n_pages, page = page_table.shape[1], cache_k.shape[2]
    topk_p = min(TOPK_MAX, seq // 4)
    topk_s = min(TOPK_MAX, (n_pages * page + dec_seq) // 4)
    t_p = bsz * seq

    xp = x_prompt.reshape(t_p, d)
    xs = x_sample.reshape(n_dec, d)
    outs = [[] for _ in range(8)]
    for i in range(depth):
        wn, wt = _pack_w_in(w_in[i])
        gqb = jnp.broadcast_to((jnp.tile(g_q[i], N_HEADS) * (ATTN_SCALE * LOG2E))[:, None], (1024, QBLK))
        gk = jnp.tile(g_k[i], KV_HEADS)[None]
        gmix = g_mix[i][None]
        gsgu = g_sgu[i][None]
        fin_w = (w_o[i].astype(BF16), g_ffn[i][None], w_up[i].astype(BF16), w_down[i].astype(BF16),
                 g_ple[i][None], w_pg[i].astype(BF16), w_p[i].astype(BF16))

        (qt, kf, kb, vf, vt, qi3t, kif, ki3, wit, vn, sga, sgu) = _proj(
            xp, wn, wt, gmix, gqb, gk, gsgu, w_s[i], jnp.transpose(b_s[i]), sample=False, tm=256)
        oatt = _attn_prompt(qt, qi3t, wit, kb.reshape(bsz, seq, 256), vt, ki3.reshape(bsz, seq, 256),
                            bsz=bsz, seq=seq, topk=topk_p)
        xp = _finish(xp, oatt.reshape(t_p, 1024), sga, sgu, p_prompt[i].reshape(t_p, -1), *fin_w,
                     tm=256, name="finish_prompt")
        outs[0].append(kf.reshape(bsz, seq, KV_HEADS, HEAD_DIM))
        outs[1].append(vf.reshape(bsz, seq, KV_HEADS, HEAD_DIM))
        outs[2].append(kif.reshape(bsz, seq, IDX_DIM))
        outs[3].append(vn.reshape(bsz, seq, d))

        w00 = jnp.repeat(w_s[i][:, 0, 0], CHUNK)[None]
        b0 = jnp.repeat(b_s[i][:, 0], CHUNK)[None]
        (qt, kf, kb, vf, vt, qi3t, kif, ki3, wit, vn, sga, sgu) = _proj(
            xs, wn, wt, gmix, gqb, gk, gsgu, w00, b0, sample=True, tm=n_dec)
        oatt = _attn_decode(page_table, qt, qi3t, wit, kf, vf, kif, cache_k[i], cache_v[i], cache_kidx[i],
                            topk=topk_s)
        xs = _finish(xs, oatt, sga, sgu, p_sample[i].reshape(n_dec, -1), *fin_w, tm=n_dec,
                     name="finish_sample")
        outs[4].append(kf.reshape(n_dec, 1, KV_HEADS, HEAD_DIM))
        outs[5].append(vf.reshape(n_dec, 1, KV_HEADS, HEAD_DIM))
        outs[6].append(kif.reshape(n_dec, 1, IDX_DIM))
        outs[7].append(vn.reshape(n_dec, 1, d))

    return (xp.reshape(bsz, seq, d), xs.reshape(n_dec, 1, d), *[jnp.stack(o) for o in outs])
```

```python
import functools
import math

import jax
import jax.numpy as jnp
from jax import lax
from jax.experimental import pallas as pl
from jax.experimental.pallas import tpu as pltpu

F32 = jnp.float32
BF16 = jnp.bfloat16
I32 = jnp.int32

N_HEADS = 16
HEAD_DIM = 64
HEAD_SHIFT = 6
KV_HEADS = 4
Q_PER_KV = N_HEADS // KV_HEADS
IDX_HEADS = 8
IDX_DIM = 64
TOPK_MAX = 256
QBLK = 128
CHUNK = 128
SGU_GROUPS = 8
EPS = 1e-6
LOG2E = math.log2(math.e)
ATTN_SCALE = HEAD_DIM ** -0.5
IDX_SCALE = IDX_DIM ** -0.5
IDX_W_SCALE = IDX_HEADS ** -0.5

LANES = 128
SUBLANES = 8
INT_MAX = 2 ** 31 - 1
KEY_NEG_INF = -(2 ** 31) + 0x7FFFFF
KEY_POS_INF = 0x7F800000
KEY_MIN_NORMAL = 0x00800000
F32_LOWEST = -3.4028234663852886e38
MASK_BIAS = -1e30
M_INIT = -1e29
VMEM_LIMIT = 56 * 1024 * 1024
SEL_ROWS = 512
SEL_GROUPS = 256
SEL_STEPS_UNCHECKED = 9
SEL_STEPS_PER_CHECK = 2
HINT_DROP = 5
DEC_SCORE_GROUP = 4
DEC_ATTN_GROUP = 2
TIE_ROWS = 256
ONES_ROWS = 16
CHUNK_UNROLL = 4
STEP_SPLIT = 1


def _dot(a, b):
    return jnp.dot(a, b, preferred_element_type=F32)


def _dot_t(a, b):
    return lax.dot_general(a, b, (((1,), (1,)), ((), ())), preferred_element_type=F32)


def _hi_lo(x):
    hi = x.astype(BF16)
    lo = (x - hi.astype(F32)).astype(BF16)
    return hi, lo


def _sigmoid(x):
    return 1.0 / (1.0 + jnp.exp(-x))


def _rms_rows(x, g):
    return x * lax.rsqrt(jnp.mean(x * x, axis=-1, keepdims=True) + EPS) * g


def _head_rms_scale(z):
    w = z.shape[1]
    seg = lax.broadcasted_iota(I32, (w, LANES), 0) >> HEAD_SHIFT
    col = lax.broadcasted_iota(I32, (w, LANES), 1)
    g = (seg == col).astype(BF16)
    row_t = lax.broadcasted_iota(I32, (LANES, w), 0)
    seg_t = lax.broadcasted_iota(I32, (LANES, w), 1) >> HEAD_SHIFT
    g_t = (row_t == seg_t).astype(BF16)
    hi, lo = _hi_lo(z * z)
    ssq = _dot(hi, g) + _dot(lo, g)
    r = lax.rsqrt(ssq * (1.0 / HEAD_DIM) + EPS)
    rh, rl = _hi_lo(r)
    return _dot(rh, g_t) + _dot(rl, g_t)


def _to_key(x):
    b = lax.bitcast_convert_type(x, I32)
    return b ^ ((b >> 31) & INT_MAX)


def _fold8(x, op):
    parts = [x[r * SUBLANES:(r + 1) * SUBLANES] for r in range(x.shape[0] // SUBLANES)]
    while len(parts) > 1:
        nxt = [op(parts[a], parts[a + 1]) for a in range(0, len(parts) - 1, 2)]
        if len(parts) % 2:
            nxt.append(parts[-1])
        parts = nxt
    return parts[0]


def _col_reduce8(x8, op):
    x8 = op(x8, pltpu.roll(x8, 4, 0))
    x8 = op(x8, pltpu.roll(x8, 2, 0))
    x8 = op(x8, pltpu.roll(x8, 1, 0))
    return x8[0:1]


_N_K, _N_V, _N_KI, _N_U, _N_VB, _N_GA, _N_GB, _N_END = 0, 256, 512, 640, 1664, 2688, 3712, 4736
_T_Q, _T_QI, _T_WI, _T_END = 0, 1024, 1536, 1552


def _proj_kernel(x_ref, wn_ref, wt_ref, gmix_ref, gqb_ref, gk_ref, gsgu_ref, ws_ref, bs_ref,
                 qt_ref, kf_ref, kb_ref, vf_ref, vt_ref, qi3t_ref, kif_ref, ki3_ref, wit_ref,
                 vn_ref, sga_ref, sgu_ref, *, sample):
    tm = x_ref.shape[0]
    nq = tm // QBLK
    xn = _rms_rows(x_ref[...], gmix_ref[...]).astype(BF16)

    def nat(lo, hi):
        return _dot(xn, wn_ref[:, lo:hi])

    def tra(lo, hi):
        return _dot_t(wt_ref[lo:hi, :], xn)

    z3 = tra(_T_Q, _T_QI).reshape(N_HEADS, HEAD_DIM, tm)
    r = lax.rsqrt(jnp.mean(z3 * z3, axis=1, keepdims=True) + EPS)
    qn = (z3 * r).reshape(N_HEADS * HEAD_DIM, tm)
    for n in range(nq):
        qt_ref[n] = (qn[:, n * QBLK:(n + 1) * QBLK] * gqb_ref[...]).astype(BF16)

    zqi = tra(_T_QI, _T_WI) * IDX_SCALE
    zero = jnp.zeros((IDX_DIM, tm), BF16)
    for h in range(IDX_HEADS):
        hq, lq = _hi_lo(zqi[h * IDX_DIM:(h + 1) * IDX_DIM])
        blk = jnp.concatenate([hq, lq, hq, zero], axis=0)
        for n in range(nq):
            qi3t_ref[n, :, h * QBLK:(h + 1) * QBLK] = blk[:, n * QBLK:(n + 1) * QBLK]
    zwi = tra(_T_WI, _T_END) * IDX_W_SCALE
    for n in range(nq):
        wit_ref[n] = zwi[0:IDX_HEADS, n * QBLK:(n + 1) * QBLK]

    zk = nat(_N_K, _N_V)
    kn = zk * _head_rms_scale(zk) * gk_ref[...]
    kb_ref[...] = kn.astype(BF16)
    zv = nat(_N_V, _N_KI)
    vt_ref[...] = zv.T.astype(BF16)
    for c in range(KV_HEADS):
        kf_ref[:, c, :] = kn[:, c * HEAD_DIM:(c + 1) * HEAD_DIM]
        vf_ref[:, c, :] = zv[:, c * HEAD_DIM:(c + 1) * HEAD_DIM]
    first = lax.broadcasted_iota(I32, (tm, LANES), 1) < IDX_DIM
    zki = nat(_N_KI, _N_U)
    kif_ref[...] = zki[:, 0:IDX_DIM]
    hk, lk = _hi_lo(zki)
    ki3_ref[:, 0:LANES] = hk
    ki3_ref[:, LANES:2 * LANES] = jnp.where(first, lk, jnp.zeros_like(lk))

    u = jax.nn.gelu(nat(_N_U, _N_VB))
    vn = _rms_rows(jax.nn.gelu(nat(_N_VB, _N_GA)), gsgu_ref[...])
    vn_ref[...] = vn
    sga_ref[...] = _sigmoid(nat(_N_GA, _N_GB))
    sgb = _sigmoid(nat(_N_GB, _N_END))
    if sample:
        sgu_ref[...] = sgb * (u * (vn * ws_ref[...] + bs_ref[...]))
    else:
        tril = (lax.broadcasted_iota(I32, (CHUNK, CHUNK), 0)
                >= lax.broadcasted_iota(I32, (CHUNK, CHUNK), 1))
        for g in range(SGU_GROUPS):
            wg = jnp.where(tril, ws_ref[g], 0.0).astype(BF16)
            bg = bs_ref[:, g:g + 1]
            cols = slice(g * CHUNK, (g + 1) * CHUNK)
            chunks = [vn[n * CHUNK:(n + 1) * CHUNK, cols].astype(BF16) for n in range(tm // CHUNK)]
            mixed = _dot(wg, jnp.concatenate(chunks, axis=1))
            for n in range(tm // CHUNK):
                rows = slice(n * CHUNK, (n + 1) * CHUNK)
                sgu_ref[rows, cols] = sgb[rows, cols] * (u[rows, cols] * (mixed[:, rows] + bg))


def _proj(x, wn, wt, gmix, gqb, gk, gsgu, ws, bs, *, sample, tm):
    t = x.shape[0]
    d = x.shape[1]
    nq = tm // QBLK
    row = lambda i: (i, 0)
    const2 = lambda i: (0, 0)
    blk3 = lambda i: (i, 0, 0)
    if sample:
        ws_spec = pl.BlockSpec(ws.shape, const2)
    else:
        ws_spec = pl.BlockSpec(ws.shape, lambda i: (0, 0, 0))
    bs_spec = pl.BlockSpec(bs.shape, const2)
    out_shapes = (
        jax.ShapeDtypeStruct((t // QBLK, 1024, QBLK), BF16),
        jax.ShapeDtypeStruct((t, KV_HEADS, HEAD_DIM), F32), jax.ShapeDtypeStruct((t, 256), BF16),
        jax.ShapeDtypeStruct((t, KV_HEADS, HEAD_DIM), F32), jax.ShapeDtypeStruct((256, t), BF16),
        jax.ShapeDtypeStruct((t // QBLK, 256, IDX_HEADS * QBLK), BF16),
        jax.ShapeDtypeStruct((t, IDX_DIM), F32),
        jax.ShapeDtypeStruct((t, 256), BF16),
        jax.ShapeDtypeStruct((t // QBLK, IDX_HEADS, QBLK), F32),
        jax.ShapeDtypeStruct((t, 1024), F32),
        jax.ShapeDtypeStruct((t, 1024), F32),
        jax.ShapeDtypeStruct((t, 1024), F32),
    )
    out_specs = (
        pl.BlockSpec((nq, 1024, QBLK), blk3),
        pl.BlockSpec((tm, KV_HEADS, HEAD_DIM), blk3), pl.BlockSpec((tm, 256), row),
        pl.BlockSpec((tm, KV_HEADS, HEAD_DIM), blk3), pl.BlockSpec((256, tm), lambda i: (0, i)),
        pl.BlockSpec((nq, 256, IDX_HEADS * QBLK), blk3),
        pl.BlockSpec((tm, IDX_DIM), row),
        pl.BlockSpec((tm, 256), row),
        pl.BlockSpec((nq, IDX_HEADS, QBLK), blk3),
        pl.BlockSpec((tm, 1024), row),
        pl.BlockSpec((tm, 1024), row),
        pl.BlockSpec((tm, 1024), row),
    )
    return pl.pallas_call(
        functools.partial(_proj_kernel, sample=sample),
        grid=(t // tm,),
        in_specs=[
            pl.BlockSpec((tm, d), row),
            pl.BlockSpec(wn.shape, const2, pipeline_mode=pl.Buffered(1)),
            pl.BlockSpec(wt.shape, const2, pipeline_mode=pl.Buffered(1)),
            pl.BlockSpec((1, d), const2), pl.BlockSpec(gqb.shape, const2),
            pl.BlockSpec((1, 256), const2), pl.BlockSpec((1, 1024), const2),
            ws_spec, bs_spec,
        ],
        out_specs=out_specs,
        out_shape=out_shapes,
        compiler_params=pltpu.CompilerParams(
            dimension_semantics=("arbitrary",), vmem_limit_bytes=VMEM_LIMIT),
        name="proj_sample" if sample else "proj_prompt",
    )(x, wn, wt, gmix, gqb, gk, gsgu, ws, bs)


def _key_to_f32(key):
    bits = key ^ ((key >> 31) & INT_MAX)
    bits = jnp.where(jnp.logical_and(bits > 0, bits < KEY_MIN_NORMAL), KEY_MIN_NORMAL, bits)
    return lax.bitcast_convert_type(bits, F32)


def _select(score_ref, n_steps, topk, lo_hint, hi_hint):
    lanes = score_ref.shape[1]
    topk_f = float(topk)

    def count(pred):
        def body(t, acc):
            start = pl.multiple_of(t * SEL_ROWS, SEL_ROWS)
            tile = score_ref[pl.ds(start, SEL_ROWS), :]
            return acc + _fold8(jnp.where(pred(tile), 1.0, 0.0), jnp.add)
        acc = lax.fori_loop(0, n_steps, body, jnp.zeros((SUBLANES, lanes), F32))
        return _col_reduce8(acc, jnp.add)

    def update(st, mid):
        lo, hi, cnt_lo, cnt_hi = st
        thr_b = jnp.broadcast_to(_key_to_f32(mid), (SEL_ROWS, lanes))
        c = count(lambda tile: tile >= thr_b)
        ge = c >= topk_f
        lo2 = jnp.where(ge, mid, lo)
        hi2 = jnp.where(c == topk_f, mid + 1, jnp.where(ge, hi, mid))
        hi2 = jnp.where(jnp.logical_and(lo2 == 0, hi2 == KEY_MIN_NORMAL), 1, hi2)
        return lo2, hi2, jnp.where(ge, c, cnt_lo), jnp.where(ge, cnt_hi, c)

    def pivot(lo, hi):
        mid = (lo >> 1) + (hi >> 1) + (lo & hi & 1)
        mid = jnp.where(jnp.logical_and(lo < 0, hi > 0), 0, mid)
        return jnp.where(jnp.logical_and(lo < KEY_MIN_NORMAL, hi > KEY_MIN_NORMAL), KEY_MIN_NORMAL, mid)

    def cond(st):
        return jnp.logical_and(st[4] > 0.5, st[5] < 48 // SEL_STEPS_PER_CHECK)

    def body(st):
        lo, hi, cnt_lo, cnt_hi = lax.fori_loop(
            0, SEL_STEPS_PER_CHECK, lambda _, s: update(s, pivot(s[0], s[1])), st[:4])
        active = jnp.where(hi != lo + 1, 1.0, 0.0)
        return lo, hi, cnt_lo, cnt_hi, jnp.max(active), st[5] + 1

    st = (jnp.full((1, lanes), KEY_NEG_INF, I32), jnp.full((1, lanes), KEY_POS_INF + 1, I32),
          jnp.full((1, lanes), 2.0 * topk_f, F32), jnp.zeros((1, lanes), F32))
    scaled = jnp.where(hi_hint > (HINT_DROP << 23) + KEY_MIN_NORMAL, hi_hint - (HINT_DROP << 23), KEY_NEG_INF)
    for hint in (lo_hint, hi_hint, scaled):
        inside = jnp.logical_and(hint > st[0], hint < st[1])
        st = update(st, jnp.where(inside, hint, pivot(st[0], st[1])))
    st = lax.fori_loop(0, SEL_STEPS_UNCHECKED, lambda _, s: update(s, pivot(s[0], s[1])), st)
    lo, _, cnt_lo, cnt_hi, _, _ = lax.while_loop(cond, body, st + (jnp.float32(1.0), jnp.int32(0)))

    tie = jnp.logical_and(cnt_lo > topk_f, lo > KEY_NEG_INF)
    any_tie = jnp.max(jnp.where(tie, 1.0, 0.0))

    @pl.when(any_tie > 0.5)
    def _():
        tri = (lax.broadcasted_iota(I32, (TIE_ROWS, TIE_ROWS), 1)
               <= lax.broadcasted_iota(I32, (TIE_ROWS, TIE_ROWS), 0)).astype(BF16)
        at_b = jnp.broadcast_to(_key_to_f32(lo), (TIE_ROWS, lanes))
        next_b = jnp.broadcast_to(_key_to_f32(lo + 1), (TIE_ROWS, lanes))
        tie_b = jnp.broadcast_to(tie, (TIE_ROWS, lanes))
        need_b = jnp.broadcast_to(topk_f - cnt_hi, (TIE_ROWS, lanes))

        def resolve(t, seen):
            for part in range(SEL_ROWS // TIE_ROWS):
                start = pl.multiple_of(t * SEL_ROWS + part * TIE_ROWS, TIE_ROWS)
                tile = score_ref[pl.ds(start, TIE_ROWS), :]
                tied = jnp.logical_and(jnp.logical_and(tile >= at_b, tile < next_b), tie_b)
                ones = jnp.where(tied, 1.0, 0.0)
                upto = _dot(tri, ones.astype(BF16)) + seen
                score_ref[pl.ds(start, TIE_ROWS), :] = jnp.where(jnp.logical_and(tied, upto > need_b), -jnp.inf, tile)
                seen = seen + _col_reduce8(_fold8(ones, jnp.add), jnp.add)
            return seen

        lax.fori_loop(0, n_steps, resolve, jnp.zeros((1, lanes), F32))

    return jnp.maximum(_key_to_f32(lo), F32_LOWEST)


def _group_hints(gm):
    lo_hint = _to_key(_col_reduce8(_fold8(gm, jnp.minimum), jnp.minimum))
    hi_hint = _to_key(_col_reduce8(_fold8(gm, jnp.maximum), jnp.maximum)) + 1
    return lo_hint, hi_hint


def _run_chunks(chunk_fn, n_ck, last_ck):
    def run(first, count):
        for k in range(count):
            chunk_fn(first + k, jnp.minimum(first + k + 1, last_ck))

    def trip(t, carry):
        run(t * CHUNK_UNROLL, CHUNK_UNROLL)
        return carry

    lax.fori_loop(0, n_ck // CHUNK_UNROLL, trip, 0)
    done = (n_ck // CHUNK_UNROLL) * CHUNK_UNROLL
    size = CHUNK_UNROLL // 2
    while size >= 1:
        @pl.when((n_ck & size) != 0)
        def _(done=done, size=size):
            run(done, size)
        done = done + (n_ck & size)
        size //= 2
def _attn_kernel(qt_ref, qi3t_ref, wit_ref, k_ref, vt_ref, ki3_ref, o_ref,
                 key_s, gm_s, wq_s, m_s, acc_s, sa_s, sb_s, la_s, lb_s, *, topk, ck):
    i = pl.program_id(1)
    n_ck = (i * QBLK + QBLK - 1) // ck + 1
    rs = 64
    gq = Q_PER_KV * QBLK

    wq_s[...] = jnp.zeros(wq_s.shape, BF16)
    for c in range(KV_HEADS):
        for g in range(Q_PER_KV):
            h = c * Q_PER_KV + g
            wq_s[c, c * HEAD_DIM:(c + 1) * HEAD_DIM, g * QBLK:(g + 1) * QBLK] = (
                qt_ref[h * HEAD_DIM:(h + 1) * HEAD_DIM, :])
    m_s[...] = jnp.full(m_s.shape, M_INIT, F32)
    acc_s[...] = jnp.zeros(acc_s.shape, F32)
    gm_s[...] = jnp.full(gm_s.shape, -jnp.inf, F32)

    s_bufs = (sa_s, sb_s)
    n_step = KV_HEADS * STEP_SPLIT
    sw = gq // STEP_SPLIT

    def qk(j, t, buf):
        c, part = divmod(t, STEP_SPLIT)
        base = pl.multiple_of(j * ck, ck)
        buf[...] = _dot(k_ref[pl.ds(base, ck), :], wq_s[c, :, part * sw:(part + 1) * sw])

    qk(0, 0, s_bufs[0])

    qpos = lax.broadcasted_iota(I32, (rs, QBLK), 1) + i * QBLK
    krow = lax.broadcasted_iota(I32, (rs, QBLK), 0)
    hk = ck // 2
    seq = k_ref.shape[0]

    def idx_dot(row0, buf):
        buf[...] = _dot(ki3_ref[pl.ds(pl.multiple_of(row0, hk), hk), :], qi3t_ref[...])

    def to_keys(row0, buf):
        for p in range(hk // rs):
            acc = jnp.zeros((rs, QBLK), F32)
            for h in range(IDX_HEADS):
                acc = acc + jnp.maximum(buf[p * rs:(p + 1) * rs, h * QBLK:(h + 1) * QBLK], 0.0) * wit_ref[h:h + 1, :]
            sc = jnp.where(krow + (row0 + p * rs) <= qpos, acc, -jnp.inf)
            key_s[pl.ds(pl.multiple_of(row0 + p * rs, rs), rs), :] = sc
            gm_s[p * rs:(p + 1) * rs, :] = jnp.maximum(gm_s[p * rs:(p + 1) * rs, :], sc)

    idx_dot(0, la_s)

    def score_chunk(j, j_next):
        base = j * ck
        idx_dot(base + hk, lb_s)
        to_keys(base, la_s)
        idx_dot(j_next * ck, la_s)
        to_keys(base + hk, lb_s)

    _run_chunks(score_chunk, n_ck, seq // ck - 1)

    lo_hint, hi_hint = _group_hints(gm_s[...])
    tau = jnp.broadcast_to(_select(key_s, n_ck * (ck // SEL_ROWS), topk, lo_hint, hi_hint), (ck, QBLK))

    last_ck = seq // ck - 1
    ones_rows = jnp.ones((ONES_ROWS, ck), BF16)

    def softmax_pv(base, t, buf, bias):
        c, part = divmod(t, STEP_SPLIT)
        cols = slice(part * sw, (part + 1) * sw)
        s = buf[...] + bias
        m_old = m_s[c, :, cols]
        m_new = jnp.maximum(m_old, jnp.max(s, axis=0, keepdims=True))
        p = jnp.exp2(s - m_new).astype(BF16)
        alpha = jnp.exp2(m_old - m_new)
        m_s[c, :, cols] = m_new
        vc = jnp.concatenate([vt_ref[c * HEAD_DIM:(c + 1) * HEAD_DIM, pl.ds(base, ck)], ones_rows], axis=0)
        acc_s[c, :, cols] = alpha * acc_s[c, :, cols] + _dot(vc, p)

    def attn_chunk(j, j_next):
        base = pl.multiple_of(j * ck, ck)
        bias = jnp.where(key_s[pl.ds(base, ck), :] >= tau, 0.0, MASK_BIAS)
        bias = jnp.concatenate([bias] * (sw // QBLK), axis=1)
        for t in range(n_step):
            if t + 1 < n_step:
                qk(j, t + 1, s_bufs[(t + 1) % 2])
            else:
                qk(j_next, 0, s_bufs[0])
            softmax_pv(base, t, s_bufs[t % 2], bias)

    _run_chunks(attn_chunk, n_ck, last_ck)

    for hp in range(N_HEADS // 2):
        parts = []
        for h in (2 * hp, 2 * hp + 1):
            c, g = divmod(h, Q_PER_KV)
            cols = slice(g * QBLK, (g + 1) * QBLK)
            parts.append(acc_s[c, 0:HEAD_DIM, cols] / acc_s[c, HEAD_DIM:HEAD_DIM + 1, cols])
        o_ref[:, hp * LANES:(hp + 1) * LANES] = jnp.concatenate(parts, axis=0).T


def _attn_prompt(qt, qi3t, wit, kb, vt, ki3, *, bsz, seq, topk, ck=512):
    nblk = seq // QBLK
    grid = (bsz, nblk)
    qblk = lambda bi, i: (bi * nblk + i, 0, 0)
    per_seq = lambda bi, i: (bi, 0, 0)
    gq = Q_PER_KV * QBLK
    return pl.pallas_call(
        functools.partial(_attn_kernel, topk=topk, ck=ck),
        grid=grid,
        in_specs=[
            pl.BlockSpec((None, 1024, QBLK), qblk),
            pl.BlockSpec((None, 256, IDX_HEADS * QBLK), qblk),
            pl.BlockSpec((None, IDX_HEADS, QBLK), qblk),
            pl.BlockSpec((None, seq, 256), per_seq, pipeline_mode=pl.Buffered(1)),
            pl.BlockSpec((256, seq), lambda bi, i: (0, bi), pipeline_mode=pl.Buffered(1)),
            pl.BlockSpec((None, seq, 256), per_seq, pipeline_mode=pl.Buffered(1)),
        ],
        out_specs=pl.BlockSpec((None, QBLK, 1024), lambda bi, i: (bi, i, 0)),
        out_shape=jax.ShapeDtypeStruct((bsz, seq, 1024), F32),
        scratch_shapes=[
            pltpu.VMEM((seq, QBLK), F32),
            pltpu.VMEM((SEL_GROUPS, QBLK), F32),
            pltpu.VMEM((KV_HEADS, 256, gq), BF16),
            pltpu.VMEM((KV_HEADS, 1, gq), F32),
            pltpu.VMEM((KV_HEADS, HEAD_DIM + ONES_ROWS, gq), F32),
            pltpu.VMEM((ck, gq // STEP_SPLIT), F32), pltpu.VMEM((ck, gq // STEP_SPLIT), F32),
            pltpu.VMEM((ck // 2, IDX_HEADS * QBLK), F32), pltpu.VMEM((ck // 2, IDX_HEADS * QBLK), F32),
        ],
        compiler_params=pltpu.CompilerParams(
            dimension_semantics=("arbitrary", "arbitrary"), vmem_limit_bytes=VMEM_LIMIT),
        name="attn_prompt",
    )(qt, qi3t, wit, kb, vt, ki3)


def _dec_score_kernel(pt_ref, qi3_ref, wcol_ref, kis_ref, *rest, n_pages, page, group):
    key_ref = rest[group * n_pages]
    del pt_ref
    n_keys = key_ref.shape[2]
    n_past = n_pages * page
    kpos = lax.broadcasted_iota(I32, (1, n_keys), 1)
    for g in range(group):
        page_refs = rest[g * n_pages:(g + 1) * n_pages]
        ki_all = jnp.concatenate(
            [r[...] for r in page_refs] + [jnp.broadcast_to(kis_ref[g], (IDX_DIM, n_keys - n_past))], axis=1)
        hk, lk = _hi_lo(ki_all)
        q3 = qi3_ref[g]
        qh = q3[:, 0:IDX_DIM]
        ql = q3[:, IDX_DIM:2 * IDX_DIM]
        lg = _dot(qh, hk) + _dot(ql, hk) + _dot(qh, lk)
        score = jnp.sum(jnp.maximum(lg, 0.0) * wcol_ref[g], axis=0, keepdims=True) + 0.0
        key_ref[g] = jnp.where(kpos <= n_past, score, -jnp.inf)


def _dec_select_kernel(key_ref, tau_ref, keyout_ref, *, topk):
    keyout_ref[...] = key_ref[...]
    gm = key_ref[0:SEL_GROUPS, :]
    for t in range(1, key_ref.shape[0] // SEL_GROUPS):
        gm = jnp.maximum(gm, key_ref[t * SEL_GROUPS:(t + 1) * SEL_GROUPS, :])
    lo_hint, hi_hint = _group_hints(gm)
    tau_ref[...] = _select(keyout_ref, key_ref.shape[0] // SEL_ROWS, topk, lo_hint, hi_hint)


def _dec_attn_kernel(pt_ref, lhs_ref, key_ref, tau_ref, ks_ref, vs_ref, *rest, n_pages, page, group):
    o_ref = rest[2 * group * n_pages]
    del pt_ref
    n_past = n_pages * page
    r_i = lax.broadcasted_iota(I32, (N_HEADS, KV_HEADS * HEAD_DIM), 0)
    l_i = lax.broadcasted_iota(I32, (N_HEADS, KV_HEADS * HEAD_DIM), 1)
    own_lanes = (l_i >> HEAD_SHIFT) == (r_i >> 2)
    for g in range(group):
        k_refs = rest[g * n_pages:(g + 1) * n_pages]
        v_refs = rest[(group + g) * n_pages:(group + g + 1) * n_pages]
        lhs = lhs_ref[g]
        bias = jnp.where(key_ref[g] >= tau_ref[g], 0.0, MASK_BIAS)
        s = jnp.concatenate([_dot(lhs, r[...].astype(BF16)) for r in k_refs], axis=1)
        s = s + bias[:, 0:n_past]
        ks = ks_ref[g].astype(BF16).astype(F32)
        s_self = jnp.sum(lhs.astype(F32) * ks, axis=1, keepdims=True) + bias[:, n_past:n_past + 1]
        m = jnp.maximum(jnp.max(s, axis=1, keepdims=True), s_self)
        p = jnp.exp2(s - m)
        p_self = jnp.exp2(s_self - m)
        l = jnp.sum(p, axis=1, keepdims=True) + p_self
        vt_all = jnp.concatenate([r[...].astype(BF16) for r in v_refs], axis=1)
        vs = vs_ref[g].astype(BF16).astype(F32)
        o = (_dot_t(p.astype(BF16), vt_all) + p_self.astype(BF16).astype(F32) * vs) / l
        o = jnp.where(own_lanes, o, 0.0)
        o_ref[g] = (o[:, 0:HEAD_DIM] + o[:, HEAD_DIM:2 * HEAD_DIM]
                    + o[:, 2 * HEAD_DIM:3 * HEAD_DIM] + o[:, 3 * HEAD_DIM:4 * HEAD_DIM])


def _attn_decode(page_table, qt_s, qi3t_s, wit_s, kf_s, vf_s, kif_s, cache_k, cache_v, cache_kidx, *, topk):
    n, n_pages = page_table.shape
    n_phys, page = cache_k.shape[0], cache_k.shape[1]
    n_keys = -(-(n_pages * page + 1) // SEL_ROWS) * SEL_ROWS
    ckt = jnp.transpose(cache_k, (0, 2, 3, 1)).reshape(n_phys, KV_HEADS * HEAD_DIM, page)
    cvt = jnp.transpose(cache_v, (0, 2, 3, 1)).reshape(n_phys, KV_HEADS * HEAD_DIM, page)
    ckit = jnp.transpose(cache_kidx, (0, 2, 1))

    def page_specs(rows, group):
        return [pl.BlockSpec((None, rows, page), lambda bi, pt, g=g, p=p: (pt[bi * group + g, p], 0, 0))
                for g in range(group) for p in range(n_pages)]

    per_step = lambda bi, *_: (bi, 0, 0)
    gs, ga = DEC_SCORE_GROUP, DEC_ATTN_GROUP
    assert n % gs == 0 and n % ga == 0
    q_s = jnp.transpose(qt_s[0])
    qi3 = jnp.transpose(qi3t_s[0].reshape(256, IDX_HEADS, n), (2, 1, 0))
    wcol = jnp.transpose(wit_s[0])[:, :, None]
    qh = q_s.reshape(n, N_HEADS, 1, HEAD_DIM)
    c_of_head = (jnp.arange(N_HEADS) // Q_PER_KV)[None, :, None, None]
    lhs = jnp.where(c_of_head == jnp.arange(KV_HEADS)[None, None, :, None], qh,
                    jnp.zeros_like(qh)).reshape(n, N_HEADS, KV_HEADS * HEAD_DIM)

    keys = pl.pallas_call(
        functools.partial(_dec_score_kernel, n_pages=n_pages, page=page, group=gs),
        grid_spec=pltpu.PrefetchScalarGridSpec(
            num_scalar_prefetch=1, grid=(n // gs,),
            in_specs=[pl.BlockSpec((gs, IDX_HEADS, 256), per_step),
                      pl.BlockSpec((gs, IDX_HEADS, 1), per_step),
                      pl.BlockSpec((gs, IDX_DIM, 1), per_step)]
            + page_specs(IDX_DIM, gs),
            out_specs=pl.BlockSpec((gs, 1, n_keys), per_step)),
        out_shape=jax.ShapeDtypeStruct((n, 1, n_keys), F32),
        compiler_params=pltpu.CompilerParams(dimension_semantics=("arbitrary",), vmem_limit_bytes=VMEM_LIMIT),
        name="dec_score",
    )(page_table, qi3, wcol, kif_s.reshape(n, IDX_DIM, 1), *([ckit] * (gs * n_pages)))

    tau, keys_t = pl.pallas_call(
        functools.partial(_dec_select_kernel, topk=topk),
        out_shape=(jax.ShapeDtypeStruct((1, n), F32), jax.ShapeDtypeStruct((n_keys, n), F32)),
        name="dec_select",
    )(jnp.transpose(keys.reshape(n, n_keys)))
    keys = jnp.transpose(keys_t).reshape(n, 1, n_keys)

    o = pl.pallas_call(
        functools.partial(_dec_attn_kernel, n_pages=n_pages, page=page, group=ga),
        grid_spec=pltpu.PrefetchScalarGridSpec(
            num_scalar_prefetch=1, grid=(n // ga,),
            in_specs=[pl.BlockSpec((ga, N_HEADS, 256), per_step),
                      pl.BlockSpec((ga, 1, n_keys), per_step),
                      pl.BlockSpec((ga, 1, 1), per_step),
                      pl.BlockSpec((ga, 1, 256), per_step),
                      pl.BlockSpec((ga, 1, 256), per_step)]
            + page_specs(KV_HEADS * HEAD_DIM, ga) + page_specs(KV_HEADS * HEAD_DIM, ga),
            out_specs=pl.BlockSpec((ga, N_HEADS, HEAD_DIM), per_step)),
        out_shape=jax.ShapeDtypeStruct((n, N_HEADS, HEAD_DIM), F32),
        compiler_params=pltpu.CompilerParams(dimension_semantics=("arbitrary",), vmem_limit_bytes=VMEM_LIMIT),
        name="dec_attn",
    )(page_table, lhs, keys, tau.reshape(n, 1, 1), kf_s.reshape(n, 1, 256), vf_s.reshape(n, 1, 256),
      *([ckt] * (ga * n_pages)), *([cvt] * (ga * n_pages)))
    return o.reshape(n, N_HEADS * HEAD_DIM)


def _finish_kernel(x_ref, oatt_ref, sga_ref, sgu_ref, p_ref, wo_ref, gffn_ref, wup_ref, wdn_ref,
                   gple_ref, wpg_ref, wp_ref, y_ref):
    merged = (sga_ref[...] * oatt_ref[...] + sgu_ref[...]).astype(BF16)
    x = x_ref[...] + _dot(merged, wo_ref[...])
    hf = _rms_rows(x, gffn_ref[...]).astype(BF16)
    up = jnp.maximum(_dot(hf, wup_ref[...]), 0.0)
    x = x + _dot((up * up).astype(BF16), wdn_ref[...])
    hp = _rms_rows(x, gple_ref[...]).astype(BF16)
    gate = _sigmoid(_dot(hp, wpg_ref[...]))
    y_ref[...] = x + gate * _dot(p_ref[...].astype(BF16), wp_ref[...])


def _finish(x, oatt, sga, sgu, p, wo, gffn, wup, wdn, gple, wpg, wp, *, tm, name):
    t, d = x.shape
    row = lambda i: (i, 0)
    const = lambda i: (0, 0)

    def wspec(w):
        return pl.BlockSpec(w.shape, const, pipeline_mode=pl.Buffered(1))

    return pl.pallas_call(
        _finish_kernel,
        grid=(t // tm,),
        in_specs=[pl.BlockSpec((tm, d), row), pl.BlockSpec((tm, d), row), pl.BlockSpec((tm, d), row),
                  pl.BlockSpec((tm, d), row), pl.BlockSpec((tm, p.shape[1]), row),
                  wspec(wo), pl.BlockSpec((1, d), const), wspec(wup), wspec(wdn),
                  pl.BlockSpec((1, d), const), wspec(wpg), wspec(wp)],
        out_specs=pl.BlockSpec((tm, d), row),
        out_shape=jax.ShapeDtypeStruct((t, d), F32),
        compiler_params=pltpu.CompilerParams(
            dimension_semantics=("arbitrary",), vmem_limit_bytes=VMEM_LIMIT),
        name=name,
    )(x, oatt, sga, sgu, p, wo, gffn, wup, wdn, gple, wpg, wp)


def _pack_w_in(w_in):
    d = w_in.shape[0]
    splits = (1024, 256, 256, 512, 64, 8, 1024, 1024, 1024, 1024)
    offs = [0]
    for n in splits:
        offs.append(offs[-1] + n)
    wq, wk, wv, wqi, wki, wwi, wu, wvb, wga, wgb = [w_in[:, offs[i]:offs[i + 1]] for i in range(10)]
    wn = jnp.concatenate([wk, wv, wki, wki, wu, wvb, wga, wgb], axis=1).astype(BF16)
    wt = jnp.concatenate([wq, wqi, wwi, jnp.zeros((d, _T_END - _T_WI - IDX_HEADS), w_in.dtype)],
                         axis=1).T.astype(BF16)
    return wn, wt


def kernel(x_prompt, x_sample, cache_k, cache_v, cache_kidx, page_table, p_prompt, p_sample, g_mix, w_in, g_q, g_k, g_sgu, w_s, b_s, w_o, g_ffn, w_up, w_down, g_ple, w_pg, w_p):
    depth = w_in.shape[0]
    bsz, seq, d = x_prompt.shape
    n_dec, dec_seq, _ = x_sample.shape
    assert dec_seq == 1 and seq % 512 == 0
    n_pages, page = page_table.shape[1], cache_k.shape[2]
    topk_p = min(TOPK_MAX, seq // 4)
    topk_s = min(TOPK_MAX, (n_pages * page + dec_seq) // 4)
    t_p = bsz * seq

    xp = x_prompt.reshape(t_p, d)
    xs = x_sample.reshape(n_dec, d)
    outs = [[] for _ in range(8)]
    for i in range(depth):
        wn, wt = _pack_w_in(w_in[i])
        gqb = jnp.broadcast_to((jnp.tile(g_q[i], N_HEADS) * (ATTN_SCALE * LOG2E))[:, None], (1024, QBLK))
        gk = jnp.tile(g_k[i], KV_HEADS)[None]
        gmix = g_mix[i][None]
        gsgu = g_sgu[i][None]
        fin_w = (w_o[i].astype(BF16), g_ffn[i][None], w_up[i].astype(BF16), w_down[i].astype(BF16),
                 g_ple[i][None], w_pg[i].astype(BF16), w_p[i].astype(BF16))

        (qt, kf, kb, vf, vt, qi3t, kif, ki3, wit, vn, sga, sgu) = _proj(
            xp, wn, wt, gmix, gqb, gk, gsgu, w_s[i], jnp.transpose(b_s[i]), sample=False, tm=256)
        oatt = _attn_prompt(qt, qi3t, wit, kb.reshape(bsz, seq, 256), vt, ki3.reshape(bsz, seq, 256),
                            bsz=bsz, seq=seq, topk=topk_p)
        xp = _finish(xp, oatt.reshape(t_p, 1024), sga, sgu, p_prompt[i].reshape(t_p, -1), *fin_w,
                     tm=256, name="finish_prompt")
        outs[0].append(kf.reshape(bsz, seq, KV_HEADS, HEAD_DIM))
        outs[1].append(vf.reshape(bsz, seq, KV_HEADS, HEAD_DIM))
        outs[2].append(kif.reshape(bsz, seq, IDX_DIM))
        outs[3].append(vn.reshape(bsz, seq, d))

        w00 = jnp.repeat(w_s[i][:, 0, 0], CHUNK)[None]
        b0 = jnp.repeat(b_s[i][:, 0], CHUNK)[None]
        (qt, kf, kb, vf, vt, qi3t, kif, ki3, wit, vn, sga, sgu) = _proj(
            xs, wn, wt, gmix, gqb, gk, gsgu, w00, b0, sample=True, tm=n_dec)
        oatt = _attn_decode(page_table, qt, qi3t, wit, kf, vf, kif, cache_k[i], cache_v[i], cache_kidx[i],
                            topk=topk_s)
        xs = _finish(xs, oatt, sga, sgu, p_sample[i].reshape(n_dec, -1), *fin_w, tm=n_dec,
                     name="finish_sample")
        outs[4].append(kf.reshape(n_dec, 1, KV_HEADS, HEAD_DIM))
        outs[5].append(vf.reshape(n_dec, 1, KV_HEADS, HEAD_DIM))
        outs[6].append(kif.reshape(n_dec, 1, IDX_DIM))
        outs[7].append(vn.reshape(n_dec, 1, d))

    return (xp.reshape(bsz, seq, d), xs.reshape(n_dec, 1, d), *[jnp.stack(o) for o in outs])
```

```python
import functools
import math

import jax
import jax.numpy as jnp
from jax import lax
from jax.experimental import pallas as pl
from jax.experimental.pallas import tpu as pltpu

F32 = jnp.float32
BF16 = jnp.bfloat16
I32 = jnp.int32

N_HEADS = 16
HEAD_DIM = 64
HEAD_SHIFT = 6
KV_HEADS = 4
Q_PER_KV = N_HEADS // KV_HEADS
IDX_HEADS = 8
IDX_DIM = 64
TOPK_MAX = 256
QBLK = 128
CHUNK = 128
SGU_GROUPS = 8
EPS = 1e-6
LOG2E = math.log2(math.e)
ATTN_SCALE = HEAD_DIM ** -0.5
IDX_SCALE = IDX_DIM ** -0.5
IDX_W_SCALE = IDX_HEADS ** -0.5

LANES = 128
SUBLANES = 8
INT_MAX = 2 ** 31 - 1
KEY_NEG_INF = -(2 ** 31) + 0x7FFFFF
KEY_POS_INF = 0x7F800000
KEY_MIN_NORMAL = 0x00800000
F32_LOWEST = -3.4028234663852886e38
MASK_BIAS = -1e30
M_INIT = -1e29
VMEM_LIMIT = 56 * 1024 * 1024
SEL_ROWS = 512
SEL_GROUPS = 256
SEL_STEPS_UNCHECKED = 9
SEL_STEPS_PER_CHECK = 2
HINT_DROP = 5
DEC_SCORE_GROUP = 4
DEC_ATTN_GROUP = 2
TIE_ROWS = 256
ONES_ROWS = 16
CHUNK_UNROLL = 4
STEP_SPLIT = 1


def _dot(a, b):
    return jnp.dot(a, b, preferred_element_type=F32)


def _dot_t(a, b):
    return lax.dot_general(a, b, (((1,), (1,)), ((), ())), preferred_element_type=F32)


def _hi_lo(x):
    hi = x.astype(BF16)
    lo = (x - hi.astype(F32)).astype(BF16)
    return hi, lo


def _sigmoid(x):
    return 1.0 / (1.0 + jnp.exp(-x))


def _rms_rows(x, g):
    return x * lax.rsqrt(jnp.mean(x * x, axis=-1, keepdims=True) + EPS) * g


def _head_rms_scale(z):
    w = z.shape[1]
    seg = lax.broadcasted_iota(I32, (w, LANES), 0) >> HEAD_SHIFT
    col = lax.broadcasted_iota(I32, (w, LANES), 1)
    g = (seg == col).astype(BF16)
    row_t = lax.broadcasted_iota(I32, (LANES, w), 0)
    seg_t = lax.broadcasted_iota(I32, (LANES, w), 1) >> HEAD_SHIFT
    g_t = (row_t == seg_t).astype(BF16)
    hi, lo = _hi_lo(z * z)
    ssq = _dot(hi, g) + _dot(lo, g)
    r = lax.rsqrt(ssq * (1.0 / HEAD_DIM) + EPS)
    rh, rl = _hi_lo(r)
    return _dot(rh, g_t) + _dot(rl, g_t)


def _to_key(x):
    b = lax.bitcast_convert_type(x, I32)
    return b ^ ((b >> 31) & INT_MAX)


def _fold8(x, op):
    parts = [x[r * SUBLANES:(r + 1) * SUBLANES] for r in range(x.shape[0] // SUBLANES)]
    while len(parts) > 1:
        nxt = [op(parts[a], parts[a + 1]) for a in range(0, len(parts) - 1, 2)]
        if len(parts) % 2:
            nxt.append(parts[-1])
        parts = nxt
    return parts[0]


def _col_reduce8(x8, op):
    x8 = op(x8, pltpu.roll(x8, 4, 0))
    x8 = op(x8, pltpu.roll(x8, 2, 0))
    x8 = op(x8, pltpu.roll(x8, 1, 0))
    return x8[0:1]


_N_K, _N_V, _N_KI, _N_U, _N_VB, _N_GA, _N_GB, _N_END = 0, 256, 512, 640, 1664, 2688, 3712, 4736
_T_Q, _T_QI, _T_WI, _T_END = 0, 1024, 1536, 1552


def _proj_kernel(x_ref, wn_ref, wt_ref, gmix_ref, gqb_ref, gk_ref, gsgu_ref, ws_ref, bs_ref,
                 qt_ref, kf_ref, kb_ref, vf_ref, vt_ref, qi3t_ref, kif_ref, ki3_ref, wit_ref,
                 vn_ref, sga_ref, sgu_ref, *, sample):
    tm = x_ref.shape[0]
    nq = tm // QBLK
    xn = _rms_rows(x_ref[...], gmix_ref[...]).astype(BF16)

    def nat(lo, hi):
        return _dot(xn, wn_ref[:, lo:hi])

    def tra(lo, hi):
        return _dot_t(wt_ref[lo:hi, :], xn)

    z3 = tra(_T_Q, _T_QI).reshape(N_HEADS, HEAD_DIM, tm)
    r = lax.rsqrt(jnp.mean(z3 * z3, axis=1, keepdims=True) + EPS)
    qn = (z3 * r).reshape(N_HEADS * HEAD_DIM, tm)
    for n in range(nq):
        qt_ref[n] = (qn[:, n * QBLK:(n + 1) * QBLK] * gqb_ref[...]).astype(BF16)

    zqi = tra(_T_QI, _T_WI) * IDX_SCALE
    zero = jnp.zeros((IDX_DIM, tm), BF16)
    for h in range(IDX_HEADS):
        hq, lq = _hi_lo(zqi[h * IDX_DIM:(h + 1) * IDX_DIM])
        blk = jnp.concatenate([hq, lq, hq, zero], axis=0)
        for n in range(nq):
            qi3t_ref[n, :, h * QBLK:(h + 1) * QBLK] = blk[:, n * QBLK:(n + 1) * QBLK]
    zwi = tra(_T_WI, _T_END) * IDX_W_SCALE
    for n in range(nq):
        wit_ref[n] = zwi[0:IDX_HEADS, n * QBLK:(n + 1) * QBLK]

    zk = nat(_N_K, _N_V)
    kn = zk * _head_rms_scale(zk) * gk_ref[...]
    kb_ref[...] = kn.astype(BF16)
    zv = nat(_N_V, _N_KI)
    vt_ref[...] = zv.T.astype(BF16)
    for c in range(KV_HEADS):
        kf_ref[:, c, :] = kn[:, c * HEAD_DIM:(c + 1) * HEAD_DIM]
        vf_ref[:, c, :] = zv[:, c * HEAD_DIM:(c + 1) * HEAD_DIM]
    first = lax.broadcasted_iota(I32, (tm, LANES), 1) < IDX_DIM
    zki = nat(_N_KI, _N_U)
    kif_ref[...] = zki[:, 0:IDX_DIM]
    hk, lk = _hi_lo(zki)
    ki3_ref[:, 0:LANES] = hk
    ki3_ref[:, LANES:2 * LANES] = jnp.where(first, lk, jnp.zeros_like(lk))

    u = jax.nn.gelu(nat(_N_U, _N_VB))
    vn = _rms_rows(jax.nn.gelu(nat(_N_VB, _N_GA)), gsgu_ref[...])
    vn_ref[...] = vn
    sga_ref[...] = _sigmoid(nat(_N_GA, _N_GB))
    sgb = _sigmoid(nat(_N_GB, _N_END))
    if sample:
        sgu_ref[...] = sgb * (u * (vn * ws_ref[...] + bs_ref[...]))
    else:
        tril = (lax.broadcasted_iota(I32, (CHUNK, CHUNK), 0)
                >= lax.broadcasted_iota(I32, (CHUNK, CHUNK), 1))
        for g in range(SGU_GROUPS):
            wg = jnp.where(tril, ws_ref[g], 0.0).astype(BF16)
            bg = bs_ref[:, g:g + 1]
            cols = slice(g * CHUNK, (g + 1) * CHUNK)
            chunks = [vn[n * CHUNK:(n + 1) * CHUNK, cols].astype(BF16) for n in range(tm // CHUNK)]
            mixed = _dot(wg, jnp.concatenate(chunks, axis=1))
            for n in range(tm // CHUNK):
                rows = slice(n * CHUNK, (n + 1) * CHUNK)
                sgu_ref[rows, cols] = sgb[rows, cols] * (u[rows, cols] * (mixed[:, rows] + bg))


def _proj(x, wn, wt, gmix, gqb, gk, gsgu, ws, bs, *, sample, tm):
    t = x.shape[0]
    d = x.shape[1]
    nq = tm // QBLK
    row = lambda i: (i, 0)
    const2 = lambda i: (0, 0)
    blk3 = lambda i: (i, 0, 0)
    if sample:
        ws_spec = pl.BlockSpec(ws.shape, const2)
    else:
        ws_spec = pl.BlockSpec(ws.shape, lambda i: (0, 0, 0))
    bs_spec = pl.BlockSpec(bs.shape, const2)
    out_shapes = (
        jax.ShapeDtypeStruct((t // QBLK, 1024, QBLK), BF16),
        jax.ShapeDtypeStruct((t, KV_HEADS, HEAD_DIM), F32), jax.ShapeDtypeStruct((t, 256), BF16),
        jax.ShapeDtypeStruct((t, KV_HEADS, HEAD_DIM), F32), jax.ShapeDtypeStruct((256, t), BF16),
        jax.ShapeDtypeStruct((t // QBLK, 256, IDX_HEADS * QBLK), BF16),
        jax.ShapeDtypeStruct((t, IDX_DIM), F32),
        jax.ShapeDtypeStruct((t, 256), BF16),
        jax.ShapeDtypeStruct((t // QBLK, IDX_HEADS, QBLK), F32),
        jax.ShapeDtypeStruct((t, 1024), F32),
        jax.ShapeDtypeStruct((t, 1024), F32),
        jax.ShapeDtypeStruct((t, 1024), F32),
    )
    out_specs = (
        pl.BlockSpec((nq, 1024, QBLK), blk3),
        pl.BlockSpec((tm, KV_HEADS, HEAD_DIM), blk3), pl.BlockSpec((tm, 256), row),
        pl.BlockSpec((tm, KV_HEADS, HEAD_DIM), blk3), pl.BlockSpec((256, tm), lambda i: (0, i)),
        pl.BlockSpec((nq, 256, IDX_HEADS * QBLK), blk3),
        pl.BlockSpec((tm, IDX_DIM), row),
        pl.BlockSpec((tm, 256), row),
        pl.BlockSpec((nq, IDX_HEADS, QBLK), blk3),
        pl.BlockSpec((tm, 1024), row),
        pl.BlockSpec((tm, 1024), row),
        pl.BlockSpec((tm, 1024), row),
    )
    return pl.pallas_call(
        functools.partial(_proj_kernel, sample=sample),
        grid=(t // tm,),
        in_specs=[
            pl.BlockSpec((tm, d), row),
            pl.BlockSpec(wn.shape, const2, pipeline_mode=pl.Buffered(1)),
            pl.BlockSpec(wt.shape, const2, pipeline_mode=pl.Buffered(1)),
            pl.BlockSpec((1, d), const2), pl.BlockSpec(gqb.shape, const2),
            pl.BlockSpec((1, 256), const2), pl.BlockSpec((1, 1024), const2),
            ws_spec, bs_spec,
        ],
        out_specs=out_specs,
        out_shape=out_shapes,
        compiler_params=pltpu.CompilerParams(
            dimension_semantics=("arbitrary",), vmem_limit_bytes=VMEM_LIMIT),
        name="proj_sample" if sample else "proj_prompt",
    )(x, wn, wt, gmix, gqb, gk, gsgu, ws, bs)


def _key_to_f32(key):
    bits = key ^ ((key >> 31) & INT_MAX)
    bits = jnp.where(jnp.logical_and(bits > 0, bits < KEY_MIN_NORMAL), KEY_MIN_NORMAL, bits)
    return lax.bitcast_convert_type(bits, F32)


def _select(score_ref, n_steps, topk, lo_hint, hi_hint):
    lanes = score_ref.shape[1]
    topk_f = float(topk)

    def count(pred):
        def body(t, acc):
            start = pl.multiple_of(t * SEL_ROWS, SEL_ROWS)
            tile = score_ref[pl.ds(start, SEL_ROWS), :]
            return acc + _fold8(jnp.where(pred(tile), 1.0, 0.0), jnp.add)
        acc = lax.fori_loop(0, n_steps, body, jnp.zeros((SUBLANES, lanes), F32))
        return _col_reduce8(acc, jnp.add)

    def update(st, mid):
        lo, hi, cnt_lo, cnt_hi = st
        thr_b = jnp.broadcast_to(_key_to_f32(mid), (SEL_ROWS, lanes))
        c = count(lambda tile: tile >= thr_b)
        ge = c >= topk_f
        lo2 = jnp.where(ge, mid, lo)
        hi2 = jnp.where(c == topk_f, mid + 1, jnp.where(ge, hi, mid))
        hi2 = jnp.where(jnp.logical_and(lo2 == 0, hi2 == KEY_MIN_NORMAL), 1, hi2)
        return lo2, hi2, jnp.where(ge, c, cnt_lo), jnp.where(ge, cnt_hi, c)

    def pivot(lo, hi):
        mid = (lo >> 1) + (hi >> 1) + (lo & hi & 1)
        mid = jnp.where(jnp.logical_and(lo < 0, hi > 0), 0, mid)
        return jnp.where(jnp.logical_and(lo < KEY_MIN_NORMAL, hi > KEY_MIN_NORMAL), KEY_MIN_NORMAL, mid)

    def cond(st):
        return jnp.logical_and(st[4] > 0.5, st[5] < 48 // SEL_STEPS_PER_CHECK)

    def body(st):
        lo, hi, cnt_lo, cnt_hi = lax.fori_loop(
            0, SEL_STEPS_PER_CHECK, lambda _, s: update(s, pivot(s[0], s[1])), st[:4])
        active = jnp.where(hi != lo + 1, 1.0, 0.0)
        return lo, hi, cnt_lo, cnt_hi, jnp.max(active), st[5] + 1

    st = (jnp.full((1, lanes), KEY_NEG_INF, I32), jnp.full((1, lanes), KEY_POS_INF + 1, I32),
          jnp.full((1, lanes), 2.0 * topk_f, F32), jnp.zeros((1, lanes), F32))
    scaled = jnp.where(hi_hint > (HINT_DROP << 23) + KEY_MIN_NORMAL, hi_hint - (HINT_DROP << 23), KEY_NEG_INF)
    for hint in (lo_hint, hi_hint, scaled):
        inside = jnp.logical_and(hint > st[0], hint < st[1])
        st = update(st, jnp.where(inside, hint, pivot(st[0], st[1])))
    st = lax.fori_loop(0, SEL_STEPS_UNCHECKED, lambda _, s: update(s, pivot(s[0], s[1])), st)
    lo, _, cnt_lo, cnt_hi, _, _ = lax.while_loop(cond, body, st + (jnp.float32(1.0), jnp.int32(0)))

    tie = jnp.logical_and(cnt_lo > topk_f, lo > KEY_NEG_INF)
    any_tie = jnp.max(jnp.where(tie, 1.0, 0.0))

    @pl.when(any_tie > 0.5)
    def _():
        tri = (lax.broadcasted_iota(I32, (TIE_ROWS, TIE_ROWS), 1)
               <= lax.broadcasted_iota(I32, (TIE_ROWS, TIE_ROWS), 0)).astype(BF16)
        at_b = jnp.broadcast_to(_key_to_f32(lo), (TIE_ROWS, lanes))
        next_b = jnp.broadcast_to(_key_to_f32(lo + 1), (TIE_ROWS, lanes))
        tie_b = jnp.broadcast_to(tie, (TIE_ROWS, lanes))
        need_b = jnp.broadcast_to(topk_f - cnt_hi, (TIE_ROWS, lanes))

        def resolve_rows(row0, n_rows, seen):
            for part in range(n_rows // TIE_ROWS):
                start = pl.multiple_of(row0 + part * TIE_ROWS, TIE_ROWS)
                tile = score_ref[pl.ds(start, TIE_ROWS), :]
                tied = jnp.logical_and(jnp.logical_and(tile >= at_b, tile < next_b), tie_b)
                ones = jnp.where(tied, 1.0, 0.0)
                upto = _dot(tri, ones.astype(BF16)) + seen
                score_ref[pl.ds(start, TIE_ROWS), :] = jnp.where(jnp.logical_and(tied, upto > need_b), -jnp.inf, tile)
                seen = seen + _col_reduce8(_fold8(ones, jnp.add), jnp.add)
            return seen

        seen = lax.fori_loop(0, n_steps // 2, lambda t, s: resolve_rows(t * 2 * SEL_ROWS, 2 * SEL_ROWS, s),
                             jnp.zeros((1, lanes), F32))

        @pl.when(n_steps % 2 == 1)
        def _():
            resolve_rows((n_steps - 1) * SEL_ROWS, SEL_ROWS, seen)

    return jnp.maximum(_key_to_f32(lo), F32_LOWEST)


def _group_hints(gm):
    lo_hint = _to_key(_col_reduce8(_fold8(gm, jnp.minimum), jnp.minimum))
    hi_hint = _to_key(_col_reduce8(_fold8(gm, jnp.maximum), jnp.maximum)) + 1
    return lo_hint, hi_hint


def _run_chunks(chunk_fn, n_ck, last_ck):
    def run(first, count):
        for k in range(count):
            chunk_fn(first + k, jnp.minimum(first + k + 1, last_ck))

    def trip(t, carry):
        run(t * CHUNK_UNROLL, CHUNK_UNROLL)
        return carry

    lax.fori_loop(0, n_ck // CHUNK_UNROLL, trip, 0)
    done = (n_ck // CHUNK_UNROLL) * CHUNK_UNROLL
    size = CHUNK_UNROLL // 2
    while size >= 1:
        @pl.when((n_ck & size) != 0)
        def _(done=done, size=size):
            run(done, size)
        done = done + (n_ck & size)
        size //= 2
def _attn_kernel(qt_ref, qi3t_ref, wit_ref, k_ref, vt_ref, ki3_ref, o_ref,
                 key_s, gm_s, wq_s, m_s, acc_s, sa_s, sb_s, la_s, lb_s, *, topk, ck):
    i = pl.program_id(1)
    n_ck = (i * QBLK + QBLK - 1) // ck + 1
    rs = 64
    gq = Q_PER_KV * QBLK

    wq_s[...] = jnp.zeros(wq_s.shape, BF16)
    for c in range(KV_HEADS):
        for g in range(Q_PER_KV):
            h = c * Q_PER_KV + g
            wq_s[c, c * HEAD_DIM:(c + 1) * HEAD_DIM, g * QBLK:(g + 1) * QBLK] = (
                qt_ref[h * HEAD_DIM:(h + 1) * HEAD_DIM, :])
    m_s[...] = jnp.full(m_s.shape, M_INIT, F32)
    acc_s[...] = jnp.zeros(acc_s.shape, F32)
    gm_s[...] = jnp.full(gm_s.shape, -jnp.inf, F32)

    s_bufs = (sa_s, sb_s)
    n_step = KV_HEADS * STEP_SPLIT
    sw = gq // STEP_SPLIT

    def qk(j, t, buf):
        c, part = divmod(t, STEP_SPLIT)
        base = pl.multiple_of(j * ck, ck)
        buf[...] = _dot(k_ref[pl.ds(base, ck), :], wq_s[c, :, part * sw:(part + 1) * sw])

    qk(0, 0, s_bufs[0])

    qpos = lax.broadcasted_iota(I32, (rs, QBLK), 1) + i * QBLK
    krow = lax.broadcasted_iota(I32, (rs, QBLK), 0)
    hk = ck // 2
    seq = k_ref.shape[0]

    def idx_dot(row0, buf):
        buf[...] = _dot(ki3_ref[pl.ds(pl.multiple_of(row0, hk), hk), :], qi3t_ref[...])

    def to_keys(row0, buf):
        for p in range(hk // rs):
            acc = jnp.zeros((rs, QBLK), F32)
            for h in range(IDX_HEADS):
                acc = acc + jnp.maximum(buf[p * rs:(p + 1) * rs, h * QBLK:(h + 1) * QBLK], 0.0) * wit_ref[h:h + 1, :]
            sc = jnp.where(krow + (row0 + p * rs) <= qpos, acc, -jnp.inf)
            key_s[pl.ds(pl.multiple_of(row0 + p * rs, rs), rs), :] = sc
            gm_s[p * rs:(p + 1) * rs, :] = jnp.maximum(gm_s[p * rs:(p + 1) * rs, :], sc)

    idx_dot(0, la_s)

    def score_chunk(j, j_next):
        base = j * ck
        idx_dot(base + hk, lb_s)
        to_keys(base, la_s)
        idx_dot(j_next * ck, la_s)
        to_keys(base + hk, lb_s)

    _run_chunks(score_chunk, n_ck, seq // ck - 1)

    lo_hint, hi_hint = _group_hints(gm_s[...])
    tau = jnp.broadcast_to(_select(key_s, n_ck * (ck // SEL_ROWS), topk, lo_hint, hi_hint), (ck, QBLK))

    last_ck = seq // ck - 1
    ones_rows = jnp.ones((ONES_ROWS, ck), BF16)

    def softmax_pv(base, t, buf, bias):
        c, part = divmod(t, STEP_SPLIT)
        cols = slice(part * sw, (part + 1) * sw)
        s = buf[...] + bias
        m_old = m_s[c, :, cols]
        m_new = jnp.maximum(m_old, jnp.max(s, axis=0, keepdims=True))
        p = jnp.exp2(s - m_new).astype(BF16)
        alpha = jnp.exp2(m_old - m_new)
        m_s[c, :, cols] = m_new
        vc = jnp.concatenate([vt_ref[c * HEAD_DIM:(c + 1) * HEAD_DIM, pl.ds(base, ck)], ones_rows], axis=0)
        acc_s[c, :, cols] = alpha * acc_s[c, :, cols] + _dot(vc, p)

    def attn_chunk(j, j_next):
        base = pl.multiple_of(j * ck, ck)
        bias = jnp.where(key_s[pl.ds(base, ck), :] >= tau, 0.0, MASK_BIAS)
        bias = jnp.concatenate([bias] * (sw // QBLK), axis=1)
        for t in range(n_step):
            if t + 1 < n_step:
                qk(j, t + 1, s_bufs[(t + 1) % 2])
            else:
                qk(j_next, 0, s_bufs[0])
            softmax_pv(base, t, s_bufs[t % 2], bias)

    _run_chunks(attn_chunk, n_ck, last_ck)

    for hp in range(N_HEADS // 2):
        parts = []
        for h in (2 * hp, 2 * hp + 1):
            c, g = divmod(h, Q_PER_KV)
            cols = slice(g * QBLK, (g + 1) * QBLK)
            parts.append(acc_s[c, 0:HEAD_DIM, cols] / acc_s[c, HEAD_DIM:HEAD_DIM + 1, cols])
        o_ref[:, hp * LANES:(hp + 1) * LANES] = jnp.concatenate(parts, axis=0).T


def _attn_prompt(qt, qi3t, wit, kb, vt, ki3, *, bsz, seq, topk, ck=512):
    nblk = seq // QBLK
    grid = (bsz, nblk)
    qblk = lambda bi, i: (bi * nblk + i, 0, 0)
    per_seq = lambda bi, i: (bi, 0, 0)
    gq = Q_PER_KV * QBLK
    return pl.pallas_call(
        functools.partial(_attn_kernel, topk=topk, ck=ck),
        grid=grid,
        in_specs=[
            pl.BlockSpec((None, 1024, QBLK), qblk),
            pl.BlockSpec((None, 256, IDX_HEADS * QBLK), qblk),
            pl.BlockSpec((None, IDX_HEADS, QBLK), qblk),
            pl.BlockSpec((None, seq, 256), per_seq, pipeline_mode=pl.Buffered(1)),
            pl.BlockSpec((256, seq), lambda bi, i: (0, bi), pipeline_mode=pl.Buffered(1)),
            pl.BlockSpec((None, seq, 256), per_seq, pipeline_mode=pl.Buffered(1)),
        ],
        out_specs=pl.BlockSpec((None, QBLK, 1024), lambda bi, i: (bi, i, 0)),
        out_shape=jax.ShapeDtypeStruct((bsz, seq, 1024), F32),
        scratch_shapes=[
            pltpu.VMEM((seq, QBLK), F32),
            pltpu.VMEM((SEL_GROUPS, QBLK), F32),
            pltpu.VMEM((KV_HEADS, 256, gq), BF16),
            pltpu.VMEM((KV_HEADS, 1, gq), F32),
            pltpu.VMEM((KV_HEADS, HEAD_DIM + ONES_ROWS, gq), F32),
            pltpu.VMEM((ck, gq // STEP_SPLIT), F32), pltpu.VMEM((ck, gq // STEP_SPLIT), F32),
            pltpu.VMEM((ck // 2, IDX_HEADS * QBLK), F32), pltpu.VMEM((ck // 2, IDX_HEADS * QBLK), F32),
        ],
        compiler_params=pltpu.CompilerParams(
            dimension_semantics=("arbitrary", "arbitrary"), vmem_limit_bytes=VMEM_LIMIT),
        name="attn_prompt",
    )(qt, qi3t, wit, kb, vt, ki3)


def _dec_score_kernel(pt_ref, qi3_ref, wcol_ref, kis_ref, *rest, n_pages, page, group):
    key_ref = rest[group * n_pages]
    del pt_ref
    n_keys = key_ref.shape[2]
    n_past = n_pages * page
    kpos = lax.broadcasted_iota(I32, (1, n_keys), 1)
    for g in range(group):
        page_refs = rest[g * n_pages:(g + 1) * n_pages]
        ki_all = jnp.concatenate(
            [r[...] for r in page_refs] + [jnp.broadcast_to(kis_ref[g], (IDX_DIM, n_keys - n_past))], axis=1)
        hk, lk = _hi_lo(ki_all)
        q3 = qi3_ref[g]
        qh = q3[:, 0:IDX_DIM]
        ql = q3[:, IDX_DIM:2 * IDX_DIM]
        lg = _dot(qh, hk) + _dot(ql, hk) + _dot(qh, lk)
        score = jnp.sum(jnp.maximum(lg, 0.0) * wcol_ref[g], axis=0, keepdims=True) + 0.0
        key_ref[g] = jnp.where(kpos <= n_past, score, -jnp.inf)


def _dec_select_kernel(key_ref, tau_ref, keyout_ref, *, topk):
    keyout_ref[...] = key_ref[...]
    gm = key_ref[0:SEL_GROUPS, :]
    for t in range(1, key_ref.shape[0] // SEL_GROUPS):
        gm = jnp.maximum(gm, key_ref[t * SEL_GROUPS:(t + 1) * SEL_GROUPS, :])
    lo_hint, hi_hint = _group_hints(gm)
    tau_ref[...] = _select(keyout_ref, key_ref.shape[0] // SEL_ROWS, topk, lo_hint, hi_hint)


def _dec_attn_kernel(pt_ref, lhs_ref, key_ref, tau_ref, ks_ref, vs_ref, *rest, n_pages, page, group):
    o_ref = rest[2 * group * n_pages]
    del pt_ref
    n_past = n_pages * page
    r_i = lax.broadcasted_iota(I32, (N_HEADS, KV_HEADS * HEAD_DIM), 0)
    l_i = lax.broadcasted_iota(I32, (N_HEADS, KV_HEADS * HEAD_DIM), 1)
    own_lanes = (l_i >> HEAD_SHIFT) == (r_i >> 2)
    for g in range(group):
        k_refs = rest[g * n_pages:(g + 1) * n_pages]
        v_refs = rest[(group + g) * n_pages:(group + g + 1) * n_pages]
        lhs = lhs_ref[g]
        bias = jnp.where(key_ref[g] >= tau_ref[g], 0.0, MASK_BIAS)
        s = jnp.concatenate([_dot(lhs, r[...].astype(BF16)) for r in k_refs], axis=1)
        s = s + bias[:, 0:n_past]
        ks = ks_ref[g].astype(BF16).astype(F32)
        s_self = jnp.sum(lhs.astype(F32) * ks, axis=1, keepdims=True) + bias[:, n_past:n_past + 1]
        m = jnp.maximum(jnp.max(s, axis=1, keepdims=True), s_self)
        p = jnp.exp2(s - m)
        p_self = jnp.exp2(s_self - m)
        l = jnp.sum(p, axis=1, keepdims=True) + p_self
        vt_all = jnp.concatenate([r[...].astype(BF16) for r in v_refs], axis=1)
        vs = vs_ref[g].astype(BF16).astype(F32)
        o = (_dot_t(p.astype(BF16), vt_all) + p_self.astype(BF16).astype(F32) * vs) / l
        o = jnp.where(own_lanes, o, 0.0)
        o_ref[g] = (o[:, 0:HEAD_DIM] + o[:, HEAD_DIM:2 * HEAD_DIM]
                    + o[:, 2 * HEAD_DIM:3 * HEAD_DIM] + o[:, 3 * HEAD_DIM:4 * HEAD_DIM])


def _attn_decode(page_table, qt_s, qi3t_s, wit_s, kf_s, vf_s, kif_s, cache_k, cache_v, cache_kidx, *, topk):
    n, n_pages = page_table.shape
    n_phys, page = cache_k.shape[0], cache_k.shape[1]
    n_keys = -(-(n_pages * page + 1) // SEL_ROWS) * SEL_ROWS
    ckt = jnp.transpose(cache_k, (0, 2, 3, 1)).reshape(n_phys, KV_HEADS * HEAD_DIM, page)
    cvt = jnp.transpose(cache_v, (0, 2, 3, 1)).reshape(n_phys, KV_HEADS * HEAD_DIM, page)
    ckit = jnp.transpose(cache_kidx, (0, 2, 1))

    def page_specs(rows, group):
        return [pl.BlockSpec((None, rows, page), lambda bi, pt, g=g, p=p: (pt[bi * group + g, p], 0, 0))
                for g in range(group) for p in range(n_pages)]

    per_step = lambda bi, *_: (bi, 0, 0)
    gs, ga = DEC_SCORE_GROUP, DEC_ATTN_GROUP
    assert n % gs == 0 and n % ga == 0
    q_s = jnp.transpose(qt_s[0])
    qi3 = jnp.transpose(qi3t_s[0].reshape(256, IDX_HEADS, n), (2, 1, 0))
    wcol = jnp.transpose(wit_s[0])[:, :, None]
    qh = q_s.reshape(n, N_HEADS, 1, HEAD_DIM)
    c_of_head = (jnp.arange(N_HEADS) // Q_PER_KV)[None, :, None, None]
    lhs = jnp.where(c_of_head == jnp.arange(KV_HEADS)[None, None, :, None], qh,
                    jnp.zeros_like(qh)).reshape(n, N_HEADS, KV_HEADS * HEAD_DIM)

    keys = pl.pallas_call(
        functools.partial(_dec_score_kernel, n_pages=n_pages, page=page, group=gs),
        grid_spec=pltpu.PrefetchScalarGridSpec(
            num_scalar_prefetch=1, grid=(n // gs,),
            in_specs=[pl.BlockSpec((gs, IDX_HEADS, 256), per_step),
                      pl.BlockSpec((gs, IDX_HEADS, 1), per_step),
                      pl.BlockSpec((gs, IDX_DIM, 1), per_step)]
            + page_specs(IDX_DIM, gs),
            out_specs=pl.BlockSpec((gs, 1, n_keys), per_step)),
        out_shape=jax.ShapeDtypeStruct((n, 1, n_keys), F32),
        compiler_params=pltpu.CompilerParams(dimension_semantics=("arbitrary",), vmem_limit_bytes=VMEM_LIMIT),
        name="dec_score",
    )(page_table, qi3, wcol, kif_s.reshape(n, IDX_DIM, 1), *([ckit] * (gs * n_pages)))

    tau, keys_t = pl.pallas_call(
        functools.partial(_dec_select_kernel, topk=topk),
        out_shape=(jax.ShapeDtypeStruct((1, n), F32), jax.ShapeDtypeStruct((n_keys, n), F32)),
        name="dec_select",
    )(jnp.transpose(keys.reshape(n, n_keys)))
    keys = jnp.transpose(keys_t).reshape(n, 1, n_keys)

    o = pl.pallas_call(
        functools.partial(_dec_attn_kernel, n_pages=n_pages, page=page, group=ga),
        grid_spec=pltpu.PrefetchScalarGridSpec(
            num_scalar_prefetch=1, grid=(n // ga,),
            in_specs=[pl.BlockSpec((ga, N_HEADS, 256), per_step),
                      pl.BlockSpec((ga, 1, n_keys), per_step),
                      pl.BlockSpec((ga, 1, 1), per_step),
                      pl.BlockSpec((ga, 1, 256), per_step),
                      pl.BlockSpec((ga, 1, 256), per_step)]
            + page_specs(KV_HEADS * HEAD_DIM, ga) + page_specs(KV_HEADS * HEAD_DIM, ga),
            out_specs=pl.BlockSpec((ga, N_HEADS, HEAD_DIM), per_step)),
        out_shape=jax.ShapeDtypeStruct((n, N_HEADS, HEAD_DIM), F32),
        compiler_params=pltpu.CompilerParams(dimension_semantics=("arbitrary",), vmem_limit_bytes=VMEM_LIMIT),
        name="dec_attn",
    )(page_table, lhs, keys, tau.reshape(n, 1, 1), kf_s.reshape(n, 1, 256), vf_s.reshape(n, 1, 256),
      *([ckt] * (ga * n_pages)), *([cvt] * (ga * n_pages)))
    return o.reshape(n, N_HEADS * HEAD_DIM)


def _finish_kernel(x_ref, oatt_ref, sga_ref, sgu_ref, p_ref, wo_ref, gffn_ref, wup_ref, wdn_ref,
                   gple_ref, wpg_ref, wp_ref, y_ref):
    merged = (sga_ref[...] * oatt_ref[...] + sgu_ref[...]).astype(BF16)
    x = x_ref[...] + _dot(merged, wo_ref[...])
    hf = _rms_rows(x, gffn_ref[...]).astype(BF16)
    up = jnp.maximum(_dot(hf, wup_ref[...]), 0.0)
    x = x + _dot((up * up).astype(BF16), wdn_ref[...])
    hp = _rms_rows(x, gple_ref[...]).astype(BF16)
    gate = _sigmoid(_dot(hp, wpg_ref[...]))
    y_ref[...] = x + gate * _dot(p_ref[...].astype(BF16), wp_ref[...])


def _finish(x, oatt, sga, sgu, p, wo, gffn, wup, wdn, gple, wpg, wp, *, tm, name):
    t, d = x.shape
    row = lambda i: (i, 0)
    const = lambda i: (0, 0)

    def wspec(w):
        return pl.BlockSpec(w.shape, const, pipeline_mode=pl.Buffered(1))

    return pl.pallas_call(
        _finish_kernel,
        grid=(t // tm,),
        in_specs=[pl.BlockSpec((tm, d), row), pl.BlockSpec((tm, d), row), pl.BlockSpec((tm, d), row),
                  pl.BlockSpec((tm, d), row), pl.BlockSpec((tm, p.shape[1]), row),
                  wspec(wo), pl.BlockSpec((1, d), const), wspec(wup), wspec(wdn),
                  pl.BlockSpec((1, d), const), wspec(wpg), wspec(wp)],
        out_specs=pl.BlockSpec((tm, d), row),
        out_shape=jax.ShapeDtypeStruct((t, d), F32),
        compiler_params=pltpu.CompilerParams(
            dimension_semantics=("arbitrary",), vmem_limit_bytes=VMEM_LIMIT),
        name=name,
    )(x, oatt, sga, sgu, p, wo, gffn, wup, wdn, gple, wpg, wp)


def _pack_w_in(w_in):
    d = w_in.shape[0]
    splits = (1024, 256, 256, 512, 64, 8, 1024, 1024, 1024, 1024)
    offs = [0]
    for n in splits:
        offs.append(offs[-1] + n)
    wq, wk, wv, wqi, wki, wwi, wu, wvb, wga, wgb = [w_in[:, offs[i]:offs[i + 1]] for i in range(10)]
    wn = jnp.concatenate([wk, wv, wki, wki, wu, wvb, wga, wgb], axis=1).astype(BF16)
    wt = jnp.concatenate([wq, wqi, wwi, jnp.zeros((d, _T_END - _T_WI - IDX_HEADS), w_in.dtype)],
                         axis=1).T.astype(BF16)
    return wn, wt


def kernel(x_prompt, x_sample, cache_k, cache_v, cache_kidx, page_table, p_prompt, p_sample, g_mix, w_in, g_q, g_k, g_sgu, w_s, b_s, w_o, g_ffn, w_up, w_down, g_ple, w_pg, w_p):
    depth = w_in.shape[0]
    bsz, seq, d = x_prompt.shape
    n_dec, dec_seq, _ = x_sample.shape
    assert dec_seq == 1 and seq % 512 == 0
    n_pages, page = page_table.shape[1], cache_k.shape[2]
    topk_p = min(TOPK_MAX, seq // 4)
    topk_s = min(TOPK_MAX, (n_pages * page + dec_seq) // 4)
    t_p = bsz * seq

    xp = x_prompt.reshape(t_p, d)
    xs = x_sample.reshape(n_dec, d)
    outs = [[] for _ in range(8)]
    for i in range(depth):
        wn, wt = _pack_w_in(w_in[i])
        gqb = jnp.broadcast_to((jnp.tile(g_q[i], N_HEADS) * (ATTN_SCALE * LOG2E))[:, None], (1024, QBLK))
        gk = jnp.tile(g_k[i], KV_HEADS)[None]
        gmix = g_mix[i][None]
        gsgu = g_sgu[i][None]
        fin_w = (w_o[i].astype(BF16), g_ffn[i][None], w_up[i].astype(BF16), w_down[i].astype(BF16),
                 g_ple[i][None], w_pg[i].astype(BF16), w_p[i].astype(BF16))

        (qt, kf, kb, vf, vt, qi3t, kif, ki3, wit, vn, sga, sgu) = _proj(
            xp, wn, wt, gmix, gqb, gk, gsgu, w_s[i], jnp.transpose(b_s[i]), sample=False, tm=256)
        oatt = _attn_prompt(qt, qi3t, wit, kb.reshape(bsz, seq, 256), vt, ki3.reshape(bsz, seq, 256),
                            bsz=bsz, seq=seq, topk=topk_p)
        xp = _finish(xp, oatt.reshape(t_p, 1024), sga, sgu, p_prompt[i].reshape(t_p, -1), *fin_w,
                     tm=256, name="finish_prompt")
        outs[0].append(kf.reshape(bsz, seq, KV_HEADS, HEAD_DIM))
        outs[1].append(vf.reshape(bsz, seq, KV_HEADS, HEAD_DIM))
        outs[2].append(kif.reshape(bsz, seq, IDX_DIM))
        outs[3].append(vn.reshape(bsz, seq, d))

        w00 = jnp.repeat(w_s[i][:, 0, 0], CHUNK)[None]
        b0 = jnp.repeat(b_s[i][:, 0], CHUNK)[None]
        (qt, kf, kb, vf, vt, qi3t, kif, ki3, wit, vn, sga, sgu) = _proj(
            xs, wn, wt, gmix, gqb, gk, gsgu, w00, b0, sample=True, tm=n_dec)
        oatt = _attn_decode(page_table, qt, qi3t, wit, kf, vf, kif, cache_k[i], cache_v[i], cache_kidx[i],
                            topk=topk_s)
        xs = _finish(xs, oatt, sga, sgu, p_sample[i].reshape(n_dec, -1), *fin_w, tm=n_dec,
                     name="finish_sample")
        outs[4].append(kf.reshape(n_dec, 1, KV_HEADS, HEAD_DIM))
        outs[5].append(vf.reshape(n_dec, 1, KV_HEADS, HEAD_DIM))
        outs[6].append(kif.reshape(n_dec, 1, IDX_DIM))
        outs[7].append(vn.reshape(n_dec, 1, d))

    return (xp.reshape(bsz, seq, d), xs.reshape(n_dec, 1, d), *[jnp.stack(o) for o in outs])
```

```python
import functools
import math

import jax
import jax.numpy as jnp
from jax import lax
from jax.experimental import pallas as pl
from jax.experimental.pallas import tpu as pltpu

F32 = jnp.float32
BF16 = jnp.bfloat16
I32 = jnp.int32

N_HEADS = 16
HEAD_DIM = 64
HEAD_SHIFT = 6
KV_HEADS = 4
Q_PER_KV = N_HEADS // KV_HEADS
IDX_HEADS = 8
IDX_DIM = 64
TOPK_MAX = 256
QBLK = 128
CHUNK = 128
SGU_GROUPS = 8
EPS = 1e-6
LOG2E = math.log2(math.e)
ATTN_SCALE = HEAD_DIM ** -0.5
IDX_SCALE = IDX_DIM ** -0.5
IDX_W_SCALE = IDX_HEADS ** -0.5

LANES = 128
SUBLANES = 8
INT_MAX = 2 ** 31 - 1
KEY_NEG_INF = -(2 ** 31) + 0x7FFFFF
KEY_POS_INF = 0x7F800000
KEY_MIN_NORMAL = 0x00800000
F32_LOWEST = -3.4028234663852886e38
MASK_BIAS = -1e30
M_INIT = -1e29
VMEM_LIMIT = 56 * 1024 * 1024
SEL_ROWS = 512
SEL_GROUPS = 256
SEL_STEPS_UNCHECKED = 11
SEL_STEPS_PER_CHECK = 2
HINT_DROP = 5
DEC_SCORE_GROUP = 4
DEC_ATTN_GROUP = 2
TIE_ROWS = 256
ONES_ROWS = 16
CHUNK_UNROLL = 8
STEP_SPLIT = 1


def _dot(a, b):
    return jnp.dot(a, b, preferred_element_type=F32)


def _dot_t(a, b):
    return lax.dot_general(a, b, (((1,), (1,)), ((), ())), preferred_element_type=F32)


def _hi_lo(x):
    hi = x.astype(BF16)
    lo = (x - hi.astype(F32)).astype(BF16)
    return hi, lo


def _sigmoid(x):
    return 1.0 / (1.0 + jnp.exp(-x))


def _rms_rows(x, g):
    return x * lax.rsqrt(jnp.mean(x * x, axis=-1, keepdims=True) + EPS) * g


def _head_rms_scale(z):
    w = z.shape[1]
    seg = lax.broadcasted_iota(I32, (w, LANES), 0) >> HEAD_SHIFT
    col = lax.broadcasted_iota(I32, (w, LANES), 1)
    g = (seg == col).astype(BF16)
    row_t = lax.broadcasted_iota(I32, (LANES, w), 0)
    seg_t = lax.broadcasted_iota(I32, (LANES, w), 1) >> HEAD_SHIFT
    g_t = (row_t == seg_t).astype(BF16)
    hi, lo = _hi_lo(z * z)
    ssq = _dot(hi, g) + _dot(lo, g)
    r = lax.rsqrt(ssq * (1.0 / HEAD_DIM) + EPS)
    rh, rl = _hi_lo(r)
    return _dot(rh, g_t) + _dot(rl, g_t)


def _to_key(x):
    b = lax.bitcast_convert_type(x, I32)
    return b ^ ((b >> 31) & INT_MAX)


def _fold8(x, op):
    parts = [x[r * SUBLANES:(r + 1) * SUBLANES] for r in range(x.shape[0] // SUBLANES)]
    while len(parts) > 1:
        nxt = [op(parts[a], parts[a + 1]) for a in range(0, len(parts) - 1, 2)]
        if len(parts) % 2:
            nxt.append(parts[-1])
        parts = nxt
    return parts[0]


def _col_reduce8(x8, op):
    x8 = op(x8, pltpu.roll(x8, 4, 0))
    x8 = op(x8, pltpu.roll(x8, 2, 0))
    x8 = op(x8, pltpu.roll(x8, 1, 0))
    return x8[0:1]


_N_K, _N_V, _N_KI, _N_U, _N_VB, _N_GA, _N_GB, _N_END = 0, 256, 512, 640, 1664, 2688, 3712, 4736
_T_Q, _T_QI, _T_WI, _T_END = 0, 1024, 1536, 1552


def _proj_kernel(x_ref, wn_ref, wt_ref, gmix_ref, gqb_ref, gk_ref, gsgu_ref, ws_ref, bs_ref,
                 qt_ref, kf_ref, kb_ref, vf_ref, vt_ref, qi3t_ref, kif_ref, ki3_ref, wit_ref,
                 vn_ref, sga_ref, sgu_ref, *, sample):
    tm = x_ref.shape[0]
    nq = tm // QBLK
    xn = _rms_rows(x_ref[...], gmix_ref[...]).astype(BF16)

    def nat(lo, hi):
        return _dot(xn, wn_ref[:, lo:hi])

    def tra(lo, hi):
        return _dot_t(wt_ref[lo:hi, :], xn)

    z3 = tra(_T_Q, _T_QI).reshape(N_HEADS, HEAD_DIM, tm)
    r = lax.rsqrt(jnp.mean(z3 * z3, axis=1, keepdims=True) + EPS)
    qn = (z3 * r).reshape(N_HEADS * HEAD_DIM, tm)
    for n in range(nq):
        qt_ref[n] = (qn[:, n * QBLK:(n + 1) * QBLK] * gqb_ref[...]).astype(BF16)

    zqi = tra(_T_QI, _T_WI) * IDX_SCALE
    zero = jnp.zeros((IDX_DIM, tm), BF16)
    for h in range(IDX_HEADS):
        hq, lq = _hi_lo(zqi[h * IDX_DIM:(h + 1) * IDX_DIM])
        blk = jnp.concatenate([hq, lq, hq, zero], axis=0)
        for n in range(nq):
            qi3t_ref[n, :, h * QBLK:(h + 1) * QBLK] = blk[:, n * QBLK:(n + 1) * QBLK]
    zwi = tra(_T_WI, _T_END) * IDX_W_SCALE
    for n in range(nq):
        wit_ref[n] = zwi[0:IDX_HEADS, n * QBLK:(n + 1) * QBLK]

    zk = nat(_N_K, _N_V)
    kn = zk * _head_rms_scale(zk) * gk_ref[...]
    kb_ref[...] = kn.astype(BF16)
    zv = nat(_N_V, _N_KI)
    vt_ref[...] = zv.T.astype(BF16)
    for c in range(KV_HEADS):
        kf_ref[:, c, :] = kn[:, c * HEAD_DIM:(c + 1) * HEAD_DIM]
        vf_ref[:, c, :] = zv[:, c * HEAD_DIM:(c + 1) * HEAD_DIM]
    first = lax.broadcasted_iota(I32, (tm, LANES), 1) < IDX_DIM
    zki = nat(_N_KI, _N_U)
    kif_ref[...] = zki[:, 0:IDX_DIM]
    hk, lk = _hi_lo(zki)
    ki3_ref[:, 0:LANES] = hk
    ki3_ref[:, LANES:2 * LANES] = jnp.where(first, lk, jnp.zeros_like(lk))

    u = jax.nn.gelu(nat(_N_U, _N_VB))
    vn = _rms_rows(jax.nn.gelu(nat(_N_VB, _N_GA)), gsgu_ref[...])
    vn_ref[...] = vn
    sga_ref[...] = _sigmoid(nat(_N_GA, _N_GB))
    sgb = _sigmoid(nat(_N_GB, _N_END))
    if sample:
        sgu_ref[...] = sgb * (u * (vn * ws_ref[...] + bs_ref[...]))
    else:
        tril = (lax.broadcasted_iota(I32, (CHUNK, CHUNK), 0)
                >= lax.broadcasted_iota(I32, (CHUNK, CHUNK), 1))
        for g in range(SGU_GROUPS):
            wg = jnp.where(tril, ws_ref[g], 0.0).astype(BF16)
            bg = bs_ref[:, g:g + 1]
            cols = slice(g * CHUNK, (g + 1) * CHUNK)
            chunks = [vn[n * CHUNK:(n + 1) * CHUNK, cols].astype(BF16) for n in range(tm // CHUNK)]
            mixed = _dot(wg, jnp.concatenate(chunks, axis=1))
            for n in range(tm // CHUNK):
                rows = slice(n * CHUNK, (n + 1) * CHUNK)
                sgu_ref[rows, cols] = sgb[rows, cols] * (u[rows, cols] * (mixed[:, rows] + bg))


def _proj(x, wn, wt, gmix, gqb, gk, gsgu, ws, bs, *, sample, tm):
    t = x.shape[0]
    d = x.shape[1]
    nq = tm // QBLK
    row = lambda i: (i, 0)
    const2 = lambda i: (0, 0)
    blk3 = lambda i: (i, 0, 0)
    if sample:
        ws_spec = pl.BlockSpec(ws.shape, const2)
    else:
        ws_spec = pl.BlockSpec(ws.shape, lambda i: (0, 0, 0))
    bs_spec = pl.BlockSpec(bs.shape, const2)
    out_shapes = (
        jax.ShapeDtypeStruct((t // QBLK, 1024, QBLK), BF16),
        jax.ShapeDtypeStruct((t, KV_HEADS, HEAD_DIM), F32), jax.ShapeDtypeStruct((t, 256), BF16),
        jax.ShapeDtypeStruct((t, KV_HEADS, HEAD_DIM), F32), jax.ShapeDtypeStruct((256, t), BF16),
        jax.ShapeDtypeStruct((t // QBLK, 256, IDX_HEADS * QBLK), BF16),
        jax.ShapeDtypeStruct((t, IDX_DIM), F32),
        jax.ShapeDtypeStruct((t, 256), BF16),
        jax.ShapeDtypeStruct((t // QBLK, IDX_HEADS, QBLK), F32),
        jax.ShapeDtypeStruct((t, 1024), F32),
        jax.ShapeDtypeStruct((t, 1024), F32),
        jax.ShapeDtypeStruct((t, 1024), F32),
    )
    out_specs = (
        pl.BlockSpec((nq, 1024, QBLK), blk3),
        pl.BlockSpec((tm, KV_HEADS, HEAD_DIM), blk3), pl.BlockSpec((tm, 256), row),
        pl.BlockSpec((tm, KV_HEADS, HEAD_DIM), blk3), pl.BlockSpec((256, tm), lambda i: (0, i)),
        pl.BlockSpec((nq, 256, IDX_HEADS * QBLK), blk3),
        pl.BlockSpec((tm, IDX_DIM), row),
        pl.BlockSpec((tm, 256), row),
        pl.BlockSpec((nq, IDX_HEADS, QBLK), blk3),
        pl.BlockSpec((tm, 1024), row),
        pl.BlockSpec((tm, 1024), row),
        pl.BlockSpec((tm, 1024), row),
    )
    return pl.pallas_call(
        functools.partial(_proj_kernel, sample=sample),
        grid=(t // tm,),
        in_specs=[
            pl.BlockSpec((tm, d), row),
            pl.BlockSpec(wn.shape, const2, pipeline_mode=pl.Buffered(1)),
            pl.BlockSpec(wt.shape, const2, pipeline_mode=pl.Buffered(1)),
            pl.BlockSpec((1, d), const2), pl.BlockSpec(gqb.shape, const2),
            pl.BlockSpec((1, 256), const2), pl.BlockSpec((1, 1024), const2),
            ws_spec, bs_spec,
        ],
        out_specs=out_specs,
        out_shape=out_shapes,
        compiler_params=pltpu.CompilerParams(
            dimension_semantics=("arbitrary",), vmem_limit_bytes=VMEM_LIMIT),
        name="proj_sample" if sample else "proj_prompt",
    )(x, wn, wt, gmix, gqb, gk, gsgu, ws, bs)


def _key_to_f32(key):
    bits = key ^ ((key >> 31) & INT_MAX)
    bits = jnp.where(jnp.logical_and(bits > 0, bits < KEY_MIN_NORMAL), KEY_MIN_NORMAL, bits)
    return lax.bitcast_convert_type(bits, F32)


def _select(score_ref, n_steps, topk, lo_hint, hi_hint):
    lanes = score_ref.shape[1]
    topk_f = float(topk)

    def count(pred):
        def body(t, acc):
            start = pl.multiple_of(t * SEL_ROWS, SEL_ROWS)
            tile = score_ref[pl.ds(start, SEL_ROWS), :]
            return acc + _fold8(jnp.where(pred(tile), 1.0, 0.0), jnp.add)
        acc = lax.fori_loop(0, n_steps, body, jnp.zeros((SUBLANES, lanes), F32))
        return _col_reduce8(acc, jnp.add)

    def update(st, mid):
        lo, hi, cnt_lo, cnt_hi = st
        thr_b = jnp.broadcast_to(_key_to_f32(mid), (SEL_ROWS, lanes))
        c = count(lambda tile: tile >= thr_b)
        ge = c >= topk_f
        lo2 = jnp.where(ge, mid, lo)
        hi2 = jnp.where(c == topk_f, mid + 1, jnp.where(ge, hi, mid))
        hi2 = jnp.where(jnp.logical_and(lo2 == 0, hi2 == KEY_MIN_NORMAL), 1, hi2)
        return lo2, hi2, jnp.where(ge, c, cnt_lo), jnp.where(ge, cnt_hi, c)

    def pivot(lo, hi):
        mid = (lo >> 1) + (hi >> 1) + (lo & hi & 1)
        mid = jnp.where(jnp.logical_and(lo < 0, hi > 0), 0, mid)
        return jnp.where(jnp.logical_and(lo < KEY_MIN_NORMAL, hi > KEY_MIN_NORMAL), KEY_MIN_NORMAL, mid)

    def cond(st):
        return jnp.logical_and(st[4] > 0.5, st[5] < 48 // SEL_STEPS_PER_CHECK)

    def body(st):
        lo, hi, cnt_lo, cnt_hi = lax.fori_loop(
            0, SEL_STEPS_PER_CHECK, lambda _, s: update(s, pivot(s[0], s[1])), st[:4])
        active = jnp.where(hi != lo + 1, 1.0, 0.0)
        return lo, hi, cnt_lo, cnt_hi, jnp.max(active), st[5] + 1

    st = (jnp.full((1, lanes), KEY_NEG_INF, I32), jnp.full((1, lanes), KEY_POS_INF + 1, I32),
          jnp.full((1, lanes), 2.0 * topk_f, F32), jnp.zeros((1, lanes), F32))
    scaled = jnp.where(hi_hint > (HINT_DROP << 23) + KEY_MIN_NORMAL, hi_hint - (HINT_DROP << 23), KEY_NEG_INF)
    for hint in (lo_hint, hi_hint, scaled):
        inside = jnp.logical_and(hint > st[0], hint < st[1])
        st = update(st, jnp.where(inside, hint, pivot(st[0], st[1])))
    st = lax.fori_loop(0, SEL_STEPS_UNCHECKED, lambda _, s: update(s, pivot(s[0], s[1])), st)
    lo, _, cnt_lo, cnt_hi, _, _ = lax.while_loop(cond, body, st + (jnp.float32(1.0), jnp.int32(0)))

    tie = jnp.logical_and(cnt_lo > topk_f, lo > KEY_NEG_INF)
    any_tie = jnp.max(jnp.where(tie, 1.0, 0.0))

    @pl.when(any_tie > 0.5)
    def _():
        tri = (lax.broadcasted_iota(I32, (TIE_ROWS, TIE_ROWS), 1)
               <= lax.broadcasted_iota(I32, (TIE_ROWS, TIE_ROWS), 0)).astype(BF16)
        at_b = jnp.broadcast_to(_key_to_f32(lo), (TIE_ROWS, lanes))
        next_b = jnp.broadcast_to(_key_to_f32(lo + 1), (TIE_ROWS, lanes))
        tie_b = jnp.broadcast_to(tie, (TIE_ROWS, lanes))
        need_b = jnp.broadcast_to(topk_f - cnt_hi, (TIE_ROWS, lanes))

        def resolve_rows(row0, n_rows, seen):
            for part in range(n_rows // TIE_ROWS):
                start = pl.multiple_of(row0 + part * TIE_ROWS, TIE_ROWS)
                tile = score_ref[pl.ds(start, TIE_ROWS), :]
                tied = jnp.logical_and(jnp.logical_and(tile >= at_b, tile < next_b), tie_b)
                ones = jnp.where(tied, 1.0, 0.0)
                upto = _dot(tri, ones.astype(BF16)) + seen
                score_ref[pl.ds(start, TIE_ROWS), :] = jnp.where(jnp.logical_and(tied, upto > need_b), -jnp.inf, tile)
                seen = seen + _col_reduce8(_fold8(ones, jnp.add), jnp.add)
            return seen

        seen = lax.fori_loop(0, n_steps // 2, lambda t, s: resolve_rows(t * 2 * SEL_ROWS, 2 * SEL_ROWS, s),
                             jnp.zeros((1, lanes), F32))

        @pl.when(n_steps % 2 == 1)
        def _():
            resolve_rows((n_steps - 1) * SEL_ROWS, SEL_ROWS, seen)

    return jnp.maximum(_key_to_f32(lo), F32_LOWEST)


def _group_hints(gm):
    lo_hint = _to_key(_col_reduce8(_fold8(gm, jnp.minimum), jnp.minimum))
    hi_hint = _to_key(_col_reduce8(_fold8(gm, jnp.maximum), jnp.maximum)) + 1
    return lo_hint, hi_hint


def _run_chunks(chunk_fn, n_ck, last_ck):
    def run(first, count):
        for k in range(count):
            chunk_fn(first + k, jnp.minimum(first + k + 1, last_ck))

    def trip(t, carry):
        run(t * CHUNK_UNROLL, CHUNK_UNROLL)
        return carry

    lax.fori_loop(0, n_ck // CHUNK_UNROLL, trip, 0)
    done = (n_ck // CHUNK_UNROLL) * CHUNK_UNROLL
    size = CHUNK_UNROLL // 2
    while size >= 1:
        @pl.when((n_ck & size) != 0)
        def _(done=done, size=size):
            run(done, size)
        done = done + (n_ck & size)
        size //= 2
def _attn_kernel(qt_ref, qi3t_ref, wit_ref, k_ref, vt_ref, ki3_ref, o_ref,
                 key_s, gm_s, wq_s, m_s, acc_s, sa_s, sb_s, la_s, lb_s, *, topk, ck):
    i = pl.program_id(1)
    n_ck = (i * QBLK + QBLK - 1) // ck + 1
    rs = 64
    gq = Q_PER_KV * QBLK

    wq_s[...] = jnp.zeros(wq_s.shape, BF16)
    for c in range(KV_HEADS):
        for g in range(Q_PER_KV):
            h = c * Q_PER_KV + g
            wq_s[c, c * HEAD_DIM:(c + 1) * HEAD_DIM, g * QBLK:(g + 1) * QBLK] = (
                qt_ref[h * HEAD_DIM:(h + 1) * HEAD_DIM, :])
    m_s[...] = jnp.full(m_s.shape, M_INIT, F32)
    acc_s[...] = jnp.zeros(acc_s.shape, F32)
    gm_s[...] = jnp.full(gm_s.shape, -jnp.inf, F32)

    s_bufs = (sa_s, sb_s)
    n_step = KV_HEADS * STEP_SPLIT
    sw = gq // STEP_SPLIT

    def qk(j, t, buf):
        c, part = divmod(t, STEP_SPLIT)
        base = pl.multiple_of(j * ck, ck)
        buf[...] = _dot(k_ref[pl.ds(base, ck), :], wq_s[c, :, part * sw:(part + 1) * sw])

    qk(0, 0, s_bufs[0])

    qpos = lax.broadcasted_iota(I32, (rs, QBLK), 1) + i * QBLK
    krow = lax.broadcasted_iota(I32, (rs, QBLK), 0)
    hk = ck // 2
    seq = k_ref.shape[0]

    def idx_dot(row0, buf):
        buf[...] = _dot(ki3_ref[pl.ds(pl.multiple_of(row0, hk), hk), :], qi3t_ref[...])

    def to_keys(row0, buf):
        for p in range(hk // rs):
            acc = jnp.zeros((rs, QBLK), F32)
            for h in range(IDX_HEADS):
                acc = acc + jnp.maximum(buf[p * rs:(p + 1) * rs, h * QBLK:(h + 1) * QBLK], 0.0) * wit_ref[h:h + 1, :]
            sc = jnp.where(krow + (row0 + p * rs) <= qpos, acc, -jnp.inf)
            key_s[pl.ds(pl.multiple_of(row0 + p * rs, rs), rs), :] = sc
            gm_s[p * rs:(p + 1) * rs, :] = jnp.maximum(gm_s[p * rs:(p + 1) * rs, :], sc)

    idx_dot(0, la_s)

    def score_chunk(j, j_next):
        base = j * ck
        idx_dot(base + hk, lb_s)
        to_keys(base, la_s)
        idx_dot(j_next * ck, la_s)
        to_keys(base + hk, lb_s)

    _run_chunks(score_chunk, n_ck, seq // ck - 1)

    lo_hint, hi_hint = _group_hints(gm_s[...])
    tau = jnp.broadcast_to(_select(key_s, n_ck * (ck // SEL_ROWS), topk, lo_hint, hi_hint), (ck, QBLK))

    last_ck = seq // ck - 1
    ones_rows = jnp.ones((ONES_ROWS, ck), BF16)

    def softmax_pv(base, t, buf, bias):
        c, part = divmod(t, STEP_SPLIT)
        cols = slice(part * sw, (part + 1) * sw)
        s = buf[...] + bias
        m_old = m_s[c, :, cols]
        m_new = jnp.maximum(m_old, jnp.max(s, axis=0, keepdims=True))
        p = jnp.exp2(s - m_new).astype(BF16)
        alpha = jnp.exp2(m_old - m_new)
        m_s[c, :, cols] = m_new
        vc = jnp.concatenate([vt_ref[c * HEAD_DIM:(c + 1) * HEAD_DIM, pl.ds(base, ck)], ones_rows], axis=0)
        acc_s[c, :, cols] = alpha * acc_s[c, :, cols] + _dot(vc, p)

    def attn_chunk(j, j_next):
        base = pl.multiple_of(j * ck, ck)
        bias = jnp.where(key_s[pl.ds(base, ck), :] >= tau, 0.0, MASK_BIAS)
        bias = jnp.concatenate([bias] * (sw // QBLK), axis=1)
        for t in range(n_step):
            if t + 1 < n_step:
                qk(j, t + 1, s_bufs[(t + 1) % 2])
            else:
                qk(j_next, 0, s_bufs[0])
            softmax_pv(base, t, s_bufs[t % 2], bias)

    _run_chunks(attn_chunk, n_ck, last_ck)

    for hp in range(N_HEADS // 2):
        parts = []
        for h in (2 * hp, 2 * hp + 1):
            c, g = divmod(h, Q_PER_KV)
            cols = slice(g * QBLK, (g + 1) * QBLK)
            parts.append(acc_s[c, 0:HEAD_DIM, cols] / acc_s[c, HEAD_DIM:HEAD_DIM + 1, cols])
        o_ref[:, hp * LANES:(hp + 1) * LANES] = jnp.concatenate(parts, axis=0).T


def _attn_prompt(qt, qi3t, wit, kb, vt, ki3, *, bsz, seq, topk, ck=512):
    nblk = seq // QBLK
    grid = (bsz, nblk)
    qblk = lambda bi, i: (bi * nblk + i, 0, 0)
    per_seq = lambda bi, i: (bi, 0, 0)
    gq = Q_PER_KV * QBLK
    return pl.pallas_call(
        functools.partial(_attn_kernel, topk=topk, ck=ck),
        grid=grid,
        in_specs=[
            pl.BlockSpec((None, 1024, QBLK), qblk),
            pl.BlockSpec((None, 256, IDX_HEADS * QBLK), qblk),
            pl.BlockSpec((None, IDX_HEADS, QBLK), qblk),
            pl.BlockSpec((None, seq, 256), per_seq, pipeline_mode=pl.Buffered(1)),
            pl.BlockSpec((256, seq), lambda bi, i: (0, bi), pipeline_mode=pl.Buffered(1)),
            pl.BlockSpec((None, seq, 256), per_seq, pipeline_mode=pl.Buffered(1)),
        ],
        out_specs=pl.BlockSpec((None, QBLK, 1024), lambda bi, i: (bi, i, 0)),
        out_shape=jax.ShapeDtypeStruct((bsz, seq, 1024), F32),
        scratch_shapes=[
            pltpu.VMEM((seq, QBLK), F32),
            pltpu.VMEM((SEL_GROUPS, QBLK), F32),
            pltpu.VMEM((KV_HEADS, 256, gq), BF16),
            pltpu.VMEM((KV_HEADS, 1, gq), F32),
            pltpu.VMEM((KV_HEADS, HEAD_DIM + ONES_ROWS, gq), F32),
            pltpu.VMEM((ck, gq // STEP_SPLIT), F32), pltpu.VMEM((ck, gq // STEP_SPLIT), F32),
            pltpu.VMEM((ck // 2, IDX_HEADS * QBLK), F32), pltpu.VMEM((ck // 2, IDX_HEADS * QBLK), F32),
        ],
        compiler_params=pltpu.CompilerParams(
            dimension_semantics=("arbitrary", "arbitrary"), vmem_limit_bytes=VMEM_LIMIT),
        name="attn_prompt",
    )(qt, qi3t, wit, kb, vt, ki3)


def _dec_score_kernel(pt_ref, qi3_ref, wcol_ref, kis_ref, *rest, n_pages, page, group):
    key_ref = rest[group * n_pages]
    del pt_ref
    n_keys = key_ref.shape[2]
    n_past = n_pages * page
    kpos = lax.broadcasted_iota(I32, (1, n_keys), 1)
    for g in range(group):
        page_refs = rest[g * n_pages:(g + 1) * n_pages]
        ki_all = jnp.concatenate(
            [r[...] for r in page_refs] + [jnp.broadcast_to(kis_ref[g], (IDX_DIM, n_keys - n_past))], axis=1)
        hk, lk = _hi_lo(ki_all)
        q3 = qi3_ref[g]
        qh = q3[:, 0:IDX_DIM]
        ql = q3[:, IDX_DIM:2 * IDX_DIM]
        lg = _dot(qh, hk) + _dot(ql, hk) + _dot(qh, lk)
        score = jnp.sum(jnp.maximum(lg, 0.0) * wcol_ref[g], axis=0, keepdims=True) + 0.0
        key_ref[g] = jnp.where(kpos <= n_past, score, -jnp.inf)


def _dec_select_kernel(key_ref, tau_ref, keyout_ref, *, topk):
    keyout_ref[...] = key_ref[...]
    gm = key_ref[0:SEL_GROUPS, :]
    for t in range(1, key_ref.shape[0] // SEL_GROUPS):
        gm = jnp.maximum(gm, key_ref[t * SEL_GROUPS:(t + 1) * SEL_GROUPS, :])
    lo_hint, hi_hint = _group_hints(gm)
    tau_ref[...] = _select(keyout_ref, key_ref.shape[0] // SEL_ROWS, topk, lo_hint, hi_hint)


def _dec_attn_kernel(pt_ref, lhs_ref, key_ref, tau_ref, ks_ref, vs_ref, *rest, n_pages, page, group):
    o_ref = rest[2 * group * n_pages]
    del pt_ref
    n_past = n_pages * page
    r_i = lax.broadcasted_iota(I32, (N_HEADS, KV_HEADS * HEAD_DIM), 0)
    l_i = lax.broadcasted_iota(I32, (N_HEADS, KV_HEADS * HEAD_DIM), 1)
    own_lanes = (l_i >> HEAD_SHIFT) == (r_i >> 2)
    for g in range(group):
        k_refs = rest[g * n_pages:(g + 1) * n_pages]
        v_refs = rest[(group + g) * n_pages:(group + g + 1) * n_pages]
        lhs = lhs_ref[g]
        bias = jnp.where(key_ref[g] >= tau_ref[g], 0.0, MASK_BIAS)
        s = jnp.concatenate([_dot(lhs, r[...].astype(BF16)) for r in k_refs], axis=1)
        s = s + bias[:, 0:n_past]
        ks = ks_ref[g].astype(BF16).astype(F32)
        s_self = jnp.sum(lhs.astype(F32) * ks, axis=1, keepdims=True) + bias[:, n_past:n_past + 1]
        m = jnp.maximum(jnp.max(s, axis=1, keepdims=True), s_self)
        p = jnp.exp2(s - m)
        p_self = jnp.exp2(s_self - m)
        l = jnp.sum(p, axis=1, keepdims=True) + p_self
        vt_all = jnp.concatenate([r[...].astype(BF16) for r in v_refs], axis=1)
        vs = vs_ref[g].astype(BF16).astype(F32)
        o = (_dot_t(p.astype(BF16), vt_all) + p_self.astype(BF16).astype(F32) * vs) / l
        o = jnp.where(own_lanes, o, 0.0)
        o_ref[g] = (o[:, 0:HEAD_DIM] + o[:, HEAD_DIM:2 * HEAD_DIM]
                    + o[:, 2 * HEAD_DIM:3 * HEAD_DIM] + o[:, 3 * HEAD_DIM:4 * HEAD_DIM])


def _attn_decode(page_table, qt_s, qi3t_s, wit_s, kf_s, vf_s, kif_s, cache_k, cache_v, cache_kidx, *, topk):
    n, n_pages = page_table.shape
    n_phys, page = cache_k.shape[0], cache_k.shape[1]
    n_keys = -(-(n_pages * page + 1) // SEL_ROWS) * SEL_ROWS
    ckt = jnp.transpose(cache_k, (0, 2, 3, 1)).reshape(n_phys, KV_HEADS * HEAD_DIM, page)
    cvt = jnp.transpose(cache_v, (0, 2, 3, 1)).reshape(n_phys, KV_HEADS * HEAD_DIM, page)
    ckit = jnp.transpose(cache_kidx, (0, 2, 1))

    def page_specs(rows, group):
        return [pl.BlockSpec((None, rows, page), lambda bi, pt, g=g, p=p: (pt[bi * group + g, p], 0, 0))
                for g in range(group) for p in range(n_pages)]

    per_step = lambda bi, *_: (bi, 0, 0)
    gs, ga = DEC_SCORE_GROUP, DEC_ATTN_GROUP
    assert n % gs == 0 and n % ga == 0
    q_s = jnp.transpose(qt_s[0])
    qi3 = jnp.transpose(qi3t_s[0].reshape(256, IDX_HEADS, n), (2, 1, 0))
    wcol = jnp.transpose(wit_s[0])[:, :, None]
    qh = q_s.reshape(n, N_HEADS, 1, HEAD_DIM)
    c_of_head = (jnp.arange(N_HEADS) // Q_PER_KV)[None, :, None, None]
    lhs = jnp.where(c_of_head == jnp.arange(KV_HEADS)[None, None, :, None], qh,
                    jnp.zeros_like(qh)).reshape(n, N_HEADS, KV_HEADS * HEAD_DIM)

    keys = pl.pallas_call(
        functools.partial(_dec_score_kernel, n_pages=n_pages, page=page, group=gs),
        grid_spec=pltpu.PrefetchScalarGridSpec(
            num_scalar_prefetch=1, grid=(n // gs,),
            in_specs=[pl.BlockSpec((gs, IDX_HEADS, 256), per_step),
                      pl.BlockSpec((gs, IDX_HEADS, 1), per_step),
                      pl.BlockSpec((gs, IDX_DIM, 1), per_step)]
            + page_specs(IDX_DIM, gs),
            out_specs=pl.BlockSpec((gs, 1, n_keys), per_step)),
        out_shape=jax.ShapeDtypeStruct((n, 1, n_keys), F32),
        compiler_params=pltpu.CompilerParams(dimension_semantics=("arbitrary",), vmem_limit_bytes=VMEM_LIMIT),
        name="dec_score",
    )(page_table, qi3, wcol, kif_s.reshape(n, IDX_DIM, 1), *([ckit] * (gs * n_pages)))

    tau, keys_t = pl.pallas_call(
        functools.partial(_dec_select_kernel, topk=topk),
        out_shape=(jax.ShapeDtypeStruct((1, n), F32), jax.ShapeDtypeStruct((n_keys, n), F32)),
        name="dec_select",
    )(jnp.transpose(keys.reshape(n, n_keys)))
    keys = jnp.transpose(keys_t).reshape(n, 1, n_keys)

    o = pl.pallas_call(
        functools.partial(_dec_attn_kernel, n_pages=n_pages, page=page, group=ga),
        grid_spec=pltpu.PrefetchScalarGridSpec(
            num_scalar_prefetch=1, grid=(n // ga,),
            in_specs=[pl.BlockSpec((ga, N_HEADS, 256), per_step),
                      pl.BlockSpec((ga, 1, n_keys), per_step),
                      pl.BlockSpec((ga, 1, 1), per_step),
                      pl.BlockSpec((ga, 1, 256), per_step),
                      pl.BlockSpec((ga, 1, 256), per_step)]
            + page_specs(KV_HEADS * HEAD_DIM, ga) + page_specs(KV_HEADS * HEAD_DIM, ga),
            out_specs=pl.BlockSpec((ga, N_HEADS, HEAD_DIM), per_step)),
        out_shape=jax.ShapeDtypeStruct((n, N_HEADS, HEAD_DIM), F32),
        compiler_params=pltpu.CompilerParams(dimension_semantics=("arbitrary",), vmem_limit_bytes=VMEM_LIMIT),
        name="dec_attn",
    )(page_table, lhs, keys, tau.reshape(n, 1, 1), kf_s.reshape(n, 1, 256), vf_s.reshape(n, 1, 256),
      *([ckt] * (ga * n_pages)), *([cvt] * (ga * n_pages)))
    return o.reshape(n, N_HEADS * HEAD_DIM)


def _finish_kernel(x_ref, oatt_ref, sga_ref, sgu_ref, p_ref, wo_ref, gffn_ref, wup_ref, wdn_ref,
                   gple_ref, wpg_ref, wp_ref, y_ref):
    merged = (sga_ref[...] * oatt_ref[...] + sgu_ref[...]).astype(BF16)
    x = x_ref[...] + _dot(merged, wo_ref[...])
    hf = _rms_rows(x, gffn_ref[...]).astype(BF16)
    up = jnp.maximum(_dot(hf, wup_ref[...]), 0.0)
    x = x + _dot((up * up).astype(BF16), wdn_ref[...])
    hp = _rms_rows(x, gple_ref[...]).astype(BF16)
    gate = _sigmoid(_dot(hp, wpg_ref[...]))
    y_ref[...] = x + gate * _dot(p_ref[...].astype(BF16), wp_ref[...])


def _finish(x, oatt, sga, sgu, p, wo, gffn, wup, wdn, gple, wpg, wp, *, tm, name):
    t, d = x.shape
    row = lambda i: (i, 0)
    const = lambda i: (0, 0)

    def wspec(w):
        return pl.BlockSpec(w.shape, const, pipeline_mode=pl.Buffered(1))

    return pl.pallas_call(
        _finish_kernel,
        grid=(t // tm,),
        in_specs=[pl.BlockSpec((tm, d), row), pl.BlockSpec((tm, d), row), pl.BlockSpec((tm, d), row),
                  pl.BlockSpec((tm, d), row), pl.BlockSpec((tm, p.shape[1]), row),
                  wspec(wo), pl.BlockSpec((1, d), const), wspec(wup), wspec(wdn),
                  pl.BlockSpec((1, d), const), wspec(wpg), wspec(wp)],
        out_specs=pl.BlockSpec((tm, d), row),
        out_shape=jax.ShapeDtypeStruct((t, d), F32),
        compiler_params=pltpu.CompilerParams(
            dimension_semantics=("arbitrary",), vmem_limit_bytes=VMEM_LIMIT),
        name=name,
    )(x, oatt, sga, sgu, p, wo, gffn, wup, wdn, gple, wpg, wp)


def _pack_w_in(w_in):
    d = w_in.shape[0]
    splits = (1024, 256, 256, 512, 64, 8, 1024, 1024, 1024, 1024)
    offs = [0]
    for n in splits:
        offs.append(offs[-1] + n)
    wq, wk, wv, wqi, wki, wwi, wu, wvb, wga, wgb = [w_in[:, offs[i]:offs[i + 1]] for i in range(10)]
    wn = jnp.concatenate([wk, wv, wki, wki, wu, wvb, wga, wgb], axis=1).astype(BF16)
    wt = jnp.concatenate([wq, wqi, wwi, jnp.zeros((d, _T_END - _T_WI - IDX_HEADS), w_in.dtype)],
                         axis=1).T.astype(BF16)
    return wn, wt


def kernel(x_prompt, x_sample, cache_k, cache_v, cache_kidx, page_table, p_prompt, p_sample, g_mix, w_in, g_q, g_k, g_sgu, w_s, b_s, w_o, g_ffn, w_up, w_down, g_ple, w_pg, w_p):
    depth = w_in.shape[0]
    bsz, seq, d = x_prompt.shape
    n_dec, dec_seq, _ = x_sample.shape
    assert dec_seq == 1 and seq % 512 == 0
    n_pages, page = page_table.shape[1], cache_k.shape[2]
    topk_p = min(TOPK_MAX, seq // 4)
    topk_s = min(TOPK_MAX, (n_pages * page + dec_seq) // 4)
    t_p = bsz * seq

    xp = x_prompt.reshape(t_p, d)
    xs = x_sample.reshape(n_dec, d)
    outs = [[] for _ in range(8)]
    for i in range(depth):
        wn, wt = _pack_w_in(w_in[i])
        gqb = jnp.broadcast_to((jnp.tile(g_q[i], N_HEADS) * (ATTN_SCALE * LOG2E))[:, None], (1024, QBLK))
        gk = jnp.tile(g_k[i], KV_HEADS)[None]
        gmix = g_mix[i][None]
        gsgu = g_sgu[i][None]
        fin_w = (w_o[i].astype(BF16), g_ffn[i][None], w_up[i].astype(BF16), w_down[i].astype(BF16),
                 g_ple[i][None], w_pg[i].astype(BF16), w_p[i].astype(BF16))

        (qt, kf, kb, vf, vt, qi3t, kif, ki3, wit, vn, sga, sgu) = _proj(
            xp, wn, wt, gmix, gqb, gk, gsgu, w_s[i], jnp.transpose(b_s[i]), sample=False, tm=256)
        oatt = _attn_prompt(qt, qi3t, wit, kb.reshape(bsz, seq, 256), vt, ki3.reshape(bsz, seq, 256),
                            bsz=bsz, seq=seq, topk=topk_p)
        xp = _finish(xp, oatt.reshape(t_p, 1024), sga, sgu, p_prompt[i].reshape(t_p, -1), *fin_w,
                     tm=256, name="finish_prompt")
        outs[0].append(kf.reshape(bsz, seq, KV_HEADS, HEAD_DIM))
        outs[1].append(vf.reshape(bsz, seq, KV_HEADS, HEAD_DIM))
        outs[2].append(kif.reshape(bsz, seq, IDX_DIM))
        outs[3].append(vn.reshape(bsz, seq, d))

        w00 = jnp.repeat(w_s[i][:, 0, 0], CHUNK)[None]
        b0 = jnp.repeat(b_s[i][:, 0], CHUNK)[None]
        (qt, kf, kb, vf, vt, qi3t, kif, ki3, wit, vn, sga, sgu) = _proj(
            xs, wn, wt, gmix, gqb, gk, gsgu, w00, b0, sample=True, tm=n_dec)
        oatt = _attn_decode(page_table, qt, qi3t, wit, kf, vf, kif, cache_k[i], cache_v[i], cache_kidx[i],
                            topk=topk_s)
        xs = _finish(xs, oatt, sga, sgu, p_sample[i].reshape(n_dec, -1), *fin_w, tm=n_dec,
                     name="finish_sample")
        outs[4].append(kf.reshape(n_dec, 1, KV_HEADS, HEAD_DIM))
        outs[5].append(vf.reshape(n_dec, 1, KV_HEADS, HEAD_DIM))
        outs[6].append(kif.reshape(n_dec, 1, IDX_DIM))
        outs[7].append(vn.reshape(n_dec, 1, d))

    return (xp.reshape(bsz, seq, d), xs.reshape(n_dec, 1, d), *[jnp.stack(o) for o in outs])
```
